```python
import math
import jax, jax.numpy as jnp
from jax import lax
import numpy as np

D_MODEL = 1024
BATCH = 8
SEQ = 2048
DEPTH = 2
DEC_BATCH = 128
DEC_SEQ = 8
PAST_LEN = 16384
PAGE_SIZE = 128

N_MEM = 256
D_CONV = D_MODEL
CONV_W = 3
D_SG = D_MODEL
SG_GROUPS = 8
SG_GROUP_DIM = D_SG // SG_GROUPS
CHUNK = 128
X_HEADS = 4
X_HEAD_DIM = D_MODEL // X_HEADS
N_KEYS = 128
N_EXPERTS = N_KEYS * N_KEYS
PEER_HEADS = 8
PEER_TOPK = 16
PEER_QDIM = 256
PEER_HALF = PEER_QDIM // 2
PEER_BLOCK = 128
EPS = 1e-6
N_IN = 3 * D_CONV + 2 * D_SG + 2 * D_MODEL
SPLITS = (D_CONV, 2 * D_CONV, 3 * D_CONV, 3 * D_CONV + D_SG, 3 * D_CONV + 2 * D_SG,
          3 * D_CONV + 2 * D_SG + D_MODEL)

kernel_name = 'hybrid_conv_sgu_peer_decoder_step'


def _rmsnorm(x, g):
    xf = x.astype(jnp.float32)
    y = xf * lax.rsqrt(jnp.mean(xf * xf, axis=-1, keepdims=True) + EPS)
    return (y * g).astype(x.dtype)


def _layernorm(x, g, b):
    xf = x.astype(jnp.float32)
    mu = jnp.mean(xf, axis=-1, keepdims=True)
    var = jnp.mean(jnp.square(xf - mu), axis=-1, keepdims=True)
    return ((xf - mu) * lax.rsqrt(var + EPS) * g + b).astype(x.dtype)


def _spatial_gate(vn, sg_w, sg_b):
    bn, L, _ = vn.shape
    c = min(CHUNK, L)
    n = L // c
    vr = vn.reshape(bn, n, c, SG_GROUPS, SG_GROUP_DIM)
    mask = jnp.tril(jnp.ones((c, c), dtype=bool))
    w = jnp.where(mask, sg_w[:, :c, :c], 0).astype(vn.dtype)
    mixed = jnp.einsum('gts,bnsgd->bntgd', w, vr) + sg_b[:, :c].T[:, :, None].astype(vn.dtype)
    return mixed.reshape(bn, L, D_SG)


def _mixer(h, conv_buf, p):
    proj = jnp.einsum('bld,dn->bln', h, p['w_in'])
    b_g, c_g, h_a, u, v, g_a, g_b = jnp.split(proj, SPLITS, axis=-1)
    z = c_g * h_a
    L = z.shape[1]
    zp = jnp.concatenate([conv_buf.astype(z.dtype), z], axis=1)
    conv = sum(p['conv_w'][j] * zp[:, j:j + L] for j in range(CONV_W))
    new_buf = zp[:, L:]
    o_a = jnp.einsum('blc,cd->bld', b_g * conv, p['w_a_out'])
    u = jax.nn.gelu(u)
    vn = _layernorm(jax.nn.gelu(v), p['sg_ln_g'], p['sg_ln_b'])
    s = _spatial_gate(vn, p['sg_w'], p['sg_b'])
    o_b = jnp.einsum('blc,cd->bld', u * s, p['w_b_out'])
    m = jax.nn.sigmoid(g_a) * o_a + jax.nn.sigmoid(g_b) * o_b
    return jnp.einsum('bld,de->ble', m, p['w_mix_o']), new_buf, vn


def _mem_kv(mem, g_mem, w_k, w_v):
    m = _rmsnorm(mem, g_mem)
    bn = mem.shape[0]
    k = jnp.einsum('bmd,de->bme', m, w_k).reshape(bn, N_MEM, X_HEADS, X_HEAD_DIM)
    v = jnp.einsum('bmd,de->bme', m, w_v).reshape(bn, N_MEM, X_HEADS, X_HEAD_DIM)
    return k, v


def _cross_attn(h, mem_k, mem_v, w_q, w_o):
    bn, L, _ = h.shape
    q = jnp.einsum('bld,de->ble', h, w_q).reshape(bn, L, X_HEADS, X_HEAD_DIM)
    s = jnp.einsum('blhe,bmhe->bhlm', q, mem_k.astype(q.dtype)).astype(jnp.float32) / math.sqrt(X_HEAD_DIM)
    pr = jax.nn.softmax(s, axis=-1).astype(h.dtype)
    o = jnp.einsum('bhlm,bmhe->blhe', pr, mem_v.astype(h.dtype)).reshape(bn, L, D_MODEL)
    return jnp.einsum('bld,de->ble', o, w_o)


def _peer(h, w_pq, sub_keys, peer_u, peer_v):
    bn, L, D = h.shape
    t = h.reshape(-1, D)
    T = t.shape[0]
    q = jnp.einsum('td,de->te', t, w_pq).reshape(T, PEER_HEADS, 2, PEER_HALF)
    s = jnp.einsum('thpk,hpnk->thpn', q, sub_keys.astype(q.dtype)).astype(jnp.float32)
    s1, i1 = lax.top_k(s[:, :, 0], PEER_TOPK)
    s2, i2 = lax.top_k(s[:, :, 1], PEER_TOPK)
    cand = (s1[..., :, None] + s2[..., None, :]).reshape(T, PEER_HEADS, PEER_TOPK * PEER_TOPK)
    cidx = (i1[..., :, None] * N_KEYS + i2[..., None, :]).reshape(T, PEER_HEADS, PEER_TOPK * PEER_TOPK)
    top_s, pos = lax.top_k(cand, PEER_TOPK)
    idx = jnp.take_along_axis(cidx, pos, axis=-1).reshape(T, PEER_HEADS * PEER_TOPK)
    g = jax.nn.softmax(top_s, axis=-1).reshape(T, PEER_HEADS * PEER_TOPK).astype(t.dtype)
    pad = (-T) % PEER_BLOCK
    tp = jnp.pad(t, ((0, pad), (0, 0))).reshape(-1, PEER_BLOCK, D)
    ip = jnp.pad(idx, ((0, pad), (0, 0))).reshape(-1, PEER_BLOCK, PEER_HEADS * PEER_TOPK)
    gp = jnp.pad(g, ((0, pad), (0, 0))).reshape(-1, PEER_BLOCK, PEER_HEADS * PEER_TOPK)

    def block(args):
        tb, ib, gb = args
        ue = jnp.take(peer_u, ib, axis=0).astype(tb.dtype)
        a = jax.nn.gelu(jnp.einsum('td,ted->te', tb, ue))
        ve = jnp.take(peer_v, ib, axis=0).astype(tb.dtype)
        return jnp.einsum('te,ted->td', gb * a, ve)

    out = lax.map(block, (tp, ip, gp)).reshape(-1, D)[:T]
    return out.reshape(bn, L, D)


def _layer(x, conv_buf, mem_k, mem_v, p):
    mo, new_buf, vn = _mixer(_rmsnorm(x, p['g_mix']), conv_buf, p)
    x = x + mo
    x = x + _cross_attn(_rmsnorm(x, p['g_xattn']), mem_k, mem_v, p['w_xq'], p['w_xo'])
    x = x + _peer(_rmsnorm(x, p['g_ffn']), p['w_pq'], p['sub_keys'], p['peer_u'], p['peer_v'])
    return x, new_buf, vn


def setup_inputs(seed: int = 0) -> dict:
    key = jax.random.key(seed)
    ks = jax.random.split(key, 32)
    f32 = jnp.float32

    def nrm(k, shape, scale):
        return jax.random.normal(k, shape, f32) * scale

    def gain(k, shape):
        return 1.0 + 0.01 * jax.random.normal(k, shape, f32)

    return {
        'x_prompt': nrm(ks[0], (BATCH, SEQ, D_MODEL), 1.0),
        'x_sample': nrm(ks[1], (DEC_BATCH, DEC_SEQ, D_MODEL), 1.0),
        'state_conv': nrm(ks[2], (DEPTH, DEC_BATCH, CONV_W - 1, D_CONV), 0.5),
        'cache_mem_k': nrm(ks[3], (DEPTH, DEC_BATCH, N_MEM, X_HEADS, X_HEAD_DIM), 1.0),
        'cache_mem_v': nrm(ks[4], (DEPTH, DEC_BATCH, N_MEM, X_HEADS, X_HEAD_DIM), 1.0),
        'mem_prompt': nrm(ks[5], (BATCH, N_MEM, D_MODEL), 1.0),
        'g_mix': gain(ks[6], (DEPTH, D_MODEL)),
        'w_in': nrm(ks[7], (DEPTH, D_MODEL, N_IN), D_MODEL ** -0.5),
        'conv_w': nrm(ks[8], (DEPTH, CONV_W, D_CONV), CONV_W ** -0.5),
        'w_a_out': nrm(ks[9], (DEPTH, D_CONV, D_MODEL), D_CONV ** -0.5),
        'sg_ln_g': gain(ks[10], (DEPTH, D_SG)),
        'sg_ln_b': nrm(ks[11], (DEPTH, D_SG), 0.01),
        'sg_w': nrm(ks[12], (DEPTH, SG_GROUPS, CHUNK, CHUNK), CHUNK ** -0.5),
        'sg_b': 1.0 + nrm(ks[13], (DEPTH, SG_GROUPS, CHUNK), 0.01),
        'w_b_out': nrm(ks[14], (DEPTH, D_SG, D_MODEL), D_SG ** -0.5),
        'w_mix_o': nrm(ks[15], (DEPTH, D_MODEL, D_MODEL), D_MODEL ** -0.5),
        'g_xattn': gain(ks[16], (DEPTH, D_MODEL)),
        'g_mem': gain(ks[17], (DEPTH, D_MODEL)),
        'w_xq': nrm(ks[18], (DEPTH, D_MODEL, D_MODEL), D_MODEL ** -0.5),
        'w_xk': nrm(ks[19], (DEPTH, D_MODEL, D_MODEL), D_MODEL ** -0.5),
        'w_xv': nrm(ks[20], (DEPTH, D_MODEL, D_MODEL), D_MODEL ** -0.5),
        'w_xo': nrm(ks[21], (DEPTH, D_MODEL, D_MODEL), D_MODEL ** -0.5),
        'g_ffn': gain(ks[22], (DEPTH, D_MODEL)),
        'w_pq': nrm(ks[23], (DEPTH, D_MODEL, PEER_HEADS * PEER_QDIM), D_MODEL ** -0.5),
        'sub_keys': nrm(ks[24], (DEPTH, PEER_HEADS, 2, N_KEYS, PEER_HALF), PEER_HALF ** -0.5),
        'peer_u': nrm(ks[25], (DEPTH, N_EXPERTS, D_MODEL), D_MODEL ** -0.5),
        'peer_v': nrm(ks[26], (DEPTH, N_EXPERTS, D_MODEL), PEER_HEADS ** -0.5),
        'g_final': gain(ks[27], (D_MODEL,)),
    }


def reference(x_prompt, x_sample, state_conv, cache_mem_k, cache_mem_v, mem_prompt,
              g_mix, w_in, conv_w, w_a_out, sg_ln_g, sg_ln_b, sg_w, sg_b, w_b_out, w_mix_o,
              g_xattn, g_mem, w_xq, w_xk, w_xv, w_xo,
              g_ffn, w_pq, sub_keys, peer_u, peer_v, g_final):
    xp, xs = x_prompt, x_sample
    conv_p, mk_p, mv_p, conv_s, sgv_s = [], [], [], [], []
    for l in range(DEPTH):
        p = dict(g_mix=g_mix[l], w_in=w_in[l], conv_w=conv_w[l], w_a_out=w_a_out[l],
                 sg_ln_g=sg_ln_g[l], sg_ln_b=sg_ln_b[l], sg_w=sg_w[l], sg_b=sg_b[l],
                 w_b_out=w_b_out[l], w_mix_o=w_mix_o[l], g_xattn=g_xattn[l],
                 w_xq=w_xq[l], w_xo=w_xo[l], g_ffn=g_ffn[l], w_pq=w_pq[l],
                 sub_keys=sub_keys[l], peer_u=peer_u[l], peer_v=peer_v[l])
        mk, mv = _mem_kv(mem_prompt, g_mem[l], w_xk[l], w_xv[l])
        zero_buf = jnp.zeros((xp.shape[0], CONV_W - 1, D_CONV), xp.dtype)
        xp, nb_p, _ = _layer(xp, zero_buf, mk, mv, p)
        conv_p.append(nb_p)
        mk_p.append(mk)
        mv_p.append(mv)
        xs, nb_s, vn_s = _layer(xs, state_conv[l], cache_mem_k[l], cache_mem_v[l], p)
        conv_s.append(nb_s)
        sgv_s.append(vn_s)
    y_prompt = _rmsnorm(xp, g_final)
    y_sample = _rmsnorm(xs, g_final)
    new_state_conv_p = jnp.stack(conv_p)
    new_mem_k_p = jnp.stack(mk_p)
    new_mem_v_p = jnp.stack(mv_p)
    new_state_conv_s = jnp.stack(conv_s)
    new_sg_v_s = jnp.stack(sgv_s)
    return (y_prompt, y_sample, new_state_conv_p, new_mem_k_p, new_mem_v_p, new_state_conv_s, new_sg_v_s)
```

```python
import functools
import math

import jax
import jax.numpy as jnp
from jax import lax
from jax.experimental import pallas as pl
from jax.experimental.pallas import tpu as pltpu

F32 = jnp.float32
BF16 = jnp.bfloat16
I32 = jnp.int32

D_MODEL = 1024
N_MEM = 256
CONV_W = 3
SG_GROUPS = 8
SG_GROUP_DIM = D_MODEL // SG_GROUPS
CHUNK = 128
X_HEADS = 4
X_HEAD_DIM = D_MODEL // X_HEADS
N_KEYS = 128
KEY_BITS = N_KEYS.bit_length() - 1
N_EXPERTS = N_KEYS * N_KEYS
PEER_HEADS = 8
PEER_TOPK = 16
PEER_HALF = 128
N_SEL = PEER_HEADS * PEER_TOPK
EPS = 1e-6

TOKEN_TILE = 256
EXPERT_TILE = 1024
VMEM_LIMIT = 56 * 1024 * 1024

_NT = (((1,), (1,)), ((), ()))


def _params(*sem):
    return pltpu.CompilerParams(dimension_semantics=sem, vmem_limit_bytes=VMEM_LIMIT)


def _const_spec(shape):
    nd = len(shape)
    return pl.BlockSpec(shape, lambda *_: (0,) * nd, pipeline_mode=pl.Buffered(1))


def _rms(x, g):
    return x * lax.rsqrt(jnp.mean(x * x, axis=-1, keepdims=True) + EPS) * g


def _gelu(x):
    c = math.sqrt(2.0 / math.pi)
    return 0.5 * x * (1.0 + jnp.tanh(c * (x + 0.044715 * (x * x * x))))


def _sigmoid(x):
    return 1.0 / (1.0 + jnp.exp(-x))


def _dot(a, b):
    return jnp.dot(a.astype(BF16), b.astype(BF16), preferred_element_type=F32)


def _dot_nt(a, b):
    return lax.dot_general(a.astype(BF16), b.astype(BF16), _NT, preferred_element_type=F32)


def _memkv_kernel(mem_ref, g_ref, wk_ref, wv_ref, k_ref, v_ref):
    m = _rms(mem_ref[...], g_ref[...]).astype(BF16)
    k_ref[...] = jnp.dot(m, wk_ref[...], preferred_element_type=F32)
    v_ref[...] = jnp.dot(m, wv_ref[...], preferred_element_type=F32)


def _mem_kv(mem, g_mem, w_k, w_v):
    t, d = mem.shape
    tm = TOKEN_TILE
    row = pl.BlockSpec((tm, d), lambda i: (i, 0))
    return pl.pallas_call(
        _memkv_kernel,
        grid=(t // tm,),
        in_specs=[row, _const_spec((1, d)), _const_spec((d, d)), _const_spec((d, d))],
        out_specs=[row, row],
        out_shape=[jax.ShapeDtypeStruct((t, d), F32)] * 2,
        compiler_params=_params("arbitrary"),
        name="mem_kv",
    )(mem, g_mem, w_k, w_v)


def _mixer_kernel(is_sample, seq_param, *refs):
    (x_ref, gmix_ref, win_ref, convw_ref, wa_ref, lng_ref, lnb_ref, sgm_ref, sgb_ref, wb_ref,
     wmo_ref) = refs[:11]
    if is_sample:
        p1_ref, p2_ref, out_ref, z_ref, vn_ref = refs[11:]
    else:
        out_ref, ztail_ref, tail_ref = refs[11:]
    d = D_MODEL
    x = x_ref[...]
    tm = x.shape[0]
    h = _rms(x, gmix_ref[...]).astype(BF16)

    def proj(k):
        return jnp.dot(h, win_ref[:, k * d:(k + 1) * d], preferred_element_type=F32)

    z = proj(1) * proj(2)
    row = lax.broadcasted_iota(I32, (tm, 1), 0)
    roll1 = pltpu.roll(z, 1, 0)
    roll2 = pltpu.roll(z, 2, 0)
    if is_sample:
        pos = row % seq_param
        prev1 = jnp.where(pos < 1, p1_ref[...], roll1)
        prev2 = jnp.where(pos < 2, p2_ref[...], roll2)
        z_ref[...] = z
    else:
        @pl.when(pl.program_id(0) % seq_param == 0)
        def _new_sequence():
            tail_ref[...] = jnp.zeros_like(tail_ref)

        t6 = tail_ref[6:7, :]
        t7 = tail_ref[7:8, :]
        prev1 = jnp.where(row < 1, t7, roll1)
        prev2 = jnp.where(row < 1, t6, jnp.where(row < 2, t7, roll2))
        tail_ref[...] = z[tm - 8:, :]
        ztail_ref[...] = z[tm - 8:, :]
    cw = convw_ref[...]
    conv = cw[0:1, :] * prev2 + cw[1:2, :] * prev1 + cw[2:3, :] * z
    o_a = _dot(proj(0) * conv, wa_ref[...])
    m = _sigmoid(proj(5)) * o_a

    v = _gelu(proj(4))
    mu = jnp.mean(v, axis=-1, keepdims=True)
    vc = v - mu
    var = jnp.mean(vc * vc, axis=-1, keepdims=True)
    vn = vc * lax.rsqrt(var + EPS) * lng_ref[...] + lnb_ref[...]
    if is_sample:
        vn_ref[...] = vn
    vnb = vn.astype(BF16)
    gd = SG_GROUP_DIM
    s = jnp.concatenate(
        [jnp.dot(sgm_ref[g], vnb[:, g * gd:(g + 1) * gd], preferred_element_type=F32)
         for g in range(SG_GROUPS)], axis=1) + sgb_ref[...]
    o_b = _dot(_gelu(proj(3)) * s, wb_ref[...])
    m = m + _sigmoid(proj(6)) * o_b

    out_ref[...] = x + _dot(m, wmo_ref[...])


def _mixer(x, w, is_sample, seq_len, state=None):
    t, d = x.shape
    tm = TOKEN_TILE
    n_tiles = t // tm
    row = pl.BlockSpec((tm, d), lambda i: (i, 0))
    in_specs = [row, _const_spec((1, d)), _const_spec((d, 7 * d)), _const_spec((CONV_W, d)),
                _const_spec((d, d)), _const_spec((1, d)), _const_spec((1, d)),
                _const_spec((SG_GROUPS, tm, tm)), _const_spec((tm, d)), _const_spec((d, d)),
                _const_spec((d, d))]
    sgm, sgb = w["sgm_s"] if is_sample else w["sgm_p"], w["sgb_s"] if is_sample else w["sgb_p"]
    args = [x, w["g_mix"], w["w_in"], w["conv_w"], w["w_a_out"], w["sg_ln_g"], w["sg_ln_b"],
            sgm, sgb, w["w_b_out"], w["w_mix_o"]]
    if is_sample:
        nb = t // seq_len
        p2 = jnp.pad(state, ((0, 0), (0, seq_len - 2), (0, 0))).reshape(t, d)
        p1 = jnp.pad(state[:, 1:2], ((0, 0), (0, seq_len - 1), (0, 0))).reshape(t, d)
        in_specs += [row, row]
        args += [p1, p2]
        out_specs = [row, row, row]
        out_shape = [jax.ShapeDtypeStruct((t, d), F32)] * 3
        scratch = []
    else:
        out_specs = [row, pl.BlockSpec((8, d), lambda i: (i, 0))]
        out_shape = [jax.ShapeDtypeStruct((t, d), F32), jax.ShapeDtypeStruct((n_tiles * 8, d), F32)]
        scratch = [pltpu.VMEM((8, d), F32)]
    outs = pl.pallas_call(
        functools.partial(_mixer_kernel, is_sample, seq_len if is_sample else seq_len // tm),
        grid=(n_tiles,),
        in_specs=in_specs,
        out_specs=out_specs,
        out_shape=out_shape,
        scratch_shapes=scratch,
        compiler_params=_params("arbitrary"),
        name="mixer_s" if is_sample else "mixer_p",
    )(*args)
    if is_sample:
        x_new, z, vn = outs
        new_buf = z.reshape(t // seq_len, seq_len, d)[:, seq_len - 2:, :]
        return x_new, new_buf, vn
    x_new, ztail = outs
    tps = seq_len // tm
    new_buf = ztail.reshape(t // seq_len, tps, 8, d)[:, tps - 1, 6:, :]
    return x_new, new_buf, None


def _attn_kernel(ns, tl, x_ref, g_ref, wq_ref, wo_ref, k_ref, v_ref, out_ref):
    x = x_ref[...]
    h = _rms(x, g_ref[...]).astype(BF16)
    q = jnp.dot(h, wq_ref[...], preferred_element_type=F32) * (1.0 / math.sqrt(X_HEAD_DIM))
    e = X_HEAD_DIM
    seqs = []
    for s in range(ns):
        qs = q[s * tl:(s + 1) * tl].astype(BF16)
        kb = k_ref[s].astype(BF16)
        vb = v_ref[s].astype(BF16)
        heads = []
        for hd in range(X_HEADS):
            sc = lax.dot_general(qs[:, hd * e:(hd + 1) * e], kb[:, hd * e:(hd + 1) * e], _NT,
                                 preferred_element_type=F32)
            p = jnp.exp(sc - jnp.max(sc, axis=-1, keepdims=True))
            p = p / jnp.sum(p, axis=-1, keepdims=True)
            heads.append(jnp.dot(p.astype(BF16), vb[:, hd * e:(hd + 1) * e],
                                 preferred_element_type=F32))
        seqs.append(jnp.concatenate(heads, axis=1))
    o = seqs[0] if ns == 1 else jnp.concatenate(seqs, axis=0)
    out_ref[...] = x + _dot(o, wo_ref[...])


def _cross_attn(x, mem_k, mem_v, w, seq_len):
    t, d = x.shape
    if seq_len >= TOKEN_TILE:
        ns, tl = 1, TOKEN_TILE
        tps = seq_len // tl
        kv_map = lambda i: (i // tps, 0, 0)
    else:
        ns, tl = 4, seq_len
        kv_map = lambda i: (i, 0, 0)
    rows = ns * tl
    row = pl.BlockSpec((rows, d), lambda i: (i, 0))
    kv = pl.BlockSpec((ns, N_MEM, d), kv_map)
    return pl.pallas_call(
        functools.partial(_attn_kernel, ns, tl),
        grid=(t // rows,),
        in_specs=[row, _const_spec((1, d)), _const_spec((d, d)), _const_spec((d, d)), kv, kv],
        out_specs=row,
        out_shape=jax.ShapeDtypeStruct((t, d), F32),
        compiler_params=_params("arbitrary"),
        name="xattn",
    )(x, w["g_xattn"], w["w_xq"], w["w_xo"], mem_k, mem_v)


def _topk_rows(s, k):
    n = s.shape[0]
    iota = lax.broadcasted_iota(I32, s.shape, 0).astype(F32)
    vals, idxs = [], []
    for _ in range(k):
        m = jnp.max(s, axis=0, keepdims=True)
        am = jnp.min(jnp.where(s == m, iota, float(n)), axis=0, keepdims=True)
        vals.append(m)
        idxs.append(am)
        s = jnp.where(iota == am, -jnp.inf, s)
    return jnp.concatenate(vals, axis=0), jnp.concatenate(idxs, axis=0)


def _route_kernel(x_ref, g_ref, wpq_ref, keys_ref, h_ref, i1_ref, i2_ref, gate_ref):
    h = _rms(x_ref[...], g_ref[...]).astype(BF16)
    h_ref[...] = h
    q = jnp.dot(h, wpq_ref[...], preferred_element_type=F32)
    k = PEER_TOPK
    sels, gates = [], []
    for hd in range(PEER_HEADS):
        half = []
        for p in range(2):
            c0 = (hd * 2 + p) * PEER_HALF
            st = lax.dot_general(keys_ref[hd * 2 + p], q[:, c0:c0 + PEER_HALF].astype(BF16), _NT,
                                 preferred_element_type=F32)
            half.append(_topk_rows(st, k))
        (s1, i1), (s2, i2) = half
        cand = jnp.concatenate([s1[a:a + 1, :] + s2 for a in range(k)], axis=0)
        cidx = jnp.concatenate([i1[a:a + 1, :] * float(N_KEYS) + i2 for a in range(k)], axis=0)
        iota = lax.broadcasted_iota(I32, cand.shape, 0).astype(F32)
        top, sel = [], []
        for _ in range(k):
            m = jnp.max(cand, axis=0, keepdims=True)
            am = jnp.min(jnp.where(cand == m, iota, float(k * k)), axis=0, keepdims=True)
            hit = iota == am
            top.append(m)
            sel.append(jnp.max(jnp.where(hit, cidx, -1.0), axis=0, keepdims=True))
            cand = jnp.where(hit, -jnp.inf, cand)
        top = jnp.concatenate(top, axis=0)
        e = jnp.exp(top - top[0:1, :])
        gates.append(e / jnp.sum(e, axis=0, keepdims=True))
        sels.append(jnp.concatenate(sel, axis=0))
    sel = jnp.concatenate(sels, axis=0).T.astype(I32)
    i1_ref[...] = sel >> KEY_BITS
    i2_ref[...] = sel & (N_KEYS - 1)
    gate_ref[...] = jnp.concatenate(gates, axis=0).T


def _peer_route(x, w):
    t, d = x.shape
    tm = TOKEN_TILE
    nq = PEER_HEADS * 2 * PEER_HALF
    row = pl.BlockSpec((tm, d), lambda i: (i, 0))
    sel = pl.BlockSpec((tm, N_SEL), lambda i: (i, 0))
    return pl.pallas_call(
        _route_kernel,
        grid=(t // tm,),
        in_specs=[row, _const_spec((1, d)), _const_spec((d, nq)),
                  _const_spec((PEER_HEADS * 2, N_KEYS, PEER_HALF))],
        out_specs=[row, sel, sel, sel],
        out_shape=[jax.ShapeDtypeStruct((t, d), BF16), jax.ShapeDtypeStruct((t, N_SEL), I32),
                   jax.ShapeDtypeStruct((t, N_SEL), I32), jax.ShapeDtypeStruct((t, N_SEL), F32)],
        compiler_params=_params("arbitrary"),
        name="peer_route",
    )(x, w["g_ffn"], w["w_pq"], w["sub_keys"])


def _expert_kernel(final_norm, x_ref, h_ref, i1_ref, i2_ref, gate_ref, u_ref, v_ref, gf_ref,
                   out_ref, q_ref, acc_ref):
    e = pl.program_id(1)
    tm = x_ref.shape[0]
    te = u_ref.shape[0]

    @pl.when(e == 0)
    def _build_q():
        acc_ref[...] = jnp.zeros_like(acc_ref)
        sub = lax.broadcasted_iota(I32, (N_KEYS, N_SEL), 0)

        def body(t, carry):
            i1 = i1_ref[pl.ds(t, 1), :]
            i2 = i2_ref[pl.ds(t, 1), :]
            g = gate_ref[pl.ds(t, 1), :]
            g_hi = g.astype(BF16)
            g_lo = (g - g_hi.astype(F32)).astype(BF16)
            one_r = jnp.where(sub == i1, 1.0, 0.0).astype(BF16)
            hit_c = sub == i2
            lhs = jnp.concatenate([one_r, one_r], axis=1)
            rhs = jnp.concatenate([jnp.where(hit_c, g_hi.astype(F32), 0.0).astype(BF16),
                                   jnp.where(hit_c, g_lo.astype(F32), 0.0).astype(BF16)], axis=1)
            qt = lax.dot_general(lhs, rhs, _NT, preferred_element_type=F32)
            q_ref[pl.ds(pl.multiple_of(t * N_KEYS, N_KEYS), N_KEYS), :] = qt
            return carry

        lax.fori_loop(0, tm, body, 0)

    rows_per_step = te // N_KEYS
    a = lax.dot_general(h_ref[...], u_ref[...], _NT, preferred_element_type=F32)
    qb = jnp.concatenate(
        [q_ref[pl.ds(e * rows_per_step + r, tm, stride=N_KEYS), :] for r in range(rows_per_step)],
        axis=1)
    p = (qb * _gelu(a)).astype(BF16)
    acc_ref[...] += jnp.dot(p, v_ref[...], preferred_element_type=F32)

    @pl.when(e == pl.num_programs(1) - 1)
    def _finish():
        y = x_ref[...] + acc_ref[...]
        if final_norm:
            y = _rms(y, gf_ref[...])
        out_ref[...] = y


def _peer_experts(x, h, i1, i2, gate, w, g_final, final_norm):
    t, d = x.shape
    tm, te = TOKEN_TILE, EXPERT_TILE
    row = pl.BlockSpec((tm, d), lambda i, e: (i, 0))
    sel = pl.BlockSpec((tm, N_SEL), lambda i, e: (i, 0))
    tab = pl.BlockSpec((te, d), lambda i, e: (e, 0))
    return pl.pallas_call(
        functools.partial(_expert_kernel, final_norm),
        grid=(t // tm, N_EXPERTS // te),
        in_specs=[row, row, sel, sel, sel, tab, tab, _const_spec((1, d))],
        out_specs=row,
        out_shape=jax.ShapeDtypeStruct((t, d), F32),
        scratch_shapes=[pltpu.VMEM((tm * N_KEYS, N_KEYS), F32), pltpu.VMEM((tm, d), F32)],
        compiler_params=_params("arbitrary", "arbitrary"),
        name="peer_experts",
    )(x, h, i1, i2, gate, w["peer_u"], w["peer_v"], g_final)


def _spatial_gate_operands(sg_w, sg_b, c, tm):
    mask = jnp.tril(jnp.ones((c, c), dtype=bool))
    wc = jnp.where(mask, sg_w[:, :c, :c], 0)
    eye = jnp.eye(tm // c, dtype=wc.dtype)
    sgm = jnp.einsum("ab,gts->gatbs", eye, wc).reshape(SG_GROUPS, tm, tm).astype(BF16)
    bias = jnp.repeat(sg_b[:, :c].T, SG_GROUP_DIM, axis=1)
    return sgm, jnp.tile(bias, (tm // c, 1))


def kernel(x_prompt, x_sample, state_conv, cache_mem_k, cache_mem_v, mem_prompt, g_mix, w_in, conv_w, w_a_out, sg_ln_g, sg_ln_b, sg_w, sg_b, w_b_out, w_mix_o, g_xattn, g_mem, w_xq, w_xk, w_xv, w_xo, g_ffn, w_pq, sub_keys, peer_u, peer_v, g_final):
    depth = g_mix.shape[0]
    bp, lp, d = x_prompt.shape
    bs, ls, _ = x_sample.shape
    xp = x_prompt.reshape(bp * lp, d)
    xs = x_sample.reshape(bs * ls, d)
    mem = mem_prompt.reshape(bp * N_MEM, d)
    gf = g_final.reshape(1, d)
    conv_p, mk_p, mv_p, conv_s, sgv_s = [], [], [], [], []
    for l in range(depth):
        sgm_p, sgb_p = _spatial_gate_operands(sg_w[l], sg_b[l], min(CHUNK, lp), TOKEN_TILE)
        sgm_s, sgb_s = _spatial_gate_operands(sg_w[l], sg_b[l], min(CHUNK, ls), TOKEN_TILE)
        w = dict(
            g_mix=g_mix[l].reshape(1, d), w_in=w_in[l].astype(BF16), conv_w=conv_w[l],
            w_a_out=w_a_out[l].astype(BF16), sg_ln_g=sg_ln_g[l].reshape(1, d),
            sg_ln_b=sg_ln_b[l].reshape(1, d), sgm_p=sgm_p, sgb_p=sgb_p, sgm_s=sgm_s, sgb_s=sgb_s,
            w_b_out=w_b_out[l].astype(BF16), w_mix_o=w_mix_o[l].astype(BF16),
            g_xattn=g_xattn[l].reshape(1, d), w_xq=w_xq[l].astype(BF16), w_xo=w_xo[l].astype(BF16),
            g_ffn=g_ffn[l].reshape(1, d), w_pq=w_pq[l].astype(BF16),
            sub_keys=sub_keys[l].reshape(PEER_HEADS * 2, N_KEYS, PEER_HALF).astype(BF16),
            peer_u=peer_u[l].astype(BF16), peer_v=peer_v[l].astype(BF16))
        last = l == depth - 1

        mk, mv = _mem_kv(mem, g_mem[l].reshape(1, d), w_xk[l].astype(BF16), w_xv[l].astype(BF16))
        mk_p.append(mk.reshape(bp, N_MEM, X_HEADS, X_HEAD_DIM))
        mv_p.append(mv.reshape(bp, N_MEM, X_HEADS, X_HEAD_DIM))

        xp, nb_p, _ = _mixer(xp, w, False, lp)
        conv_p.append(nb_p)
        xp = _cross_attn(xp, mk.reshape(bp, N_MEM, d), mv.reshape(bp, N_MEM, d), w, lp)
        xp = _peer_experts(xp, *_peer_route(xp, w), w, gf, last)

        xs, nb_s, vn_s = _mixer(xs, w, True, ls, state_conv[l])
        conv_s.append(nb_s)
        sgv_s.append(vn_s.reshape(bs, ls, d))
        xs = _cross_attn(xs, cache_mem_k[l].reshape(bs, N_MEM, d), cache_mem_v[l].reshape(bs, N_MEM, d),
                         w, ls)
        xs = _peer_experts(xs, *_peer_route(xs, w), w, gf, last)

    return (xp.reshape(bp, lp, d), xs.reshape(bs, ls, d), jnp.stack(conv_p), jnp.stack(mk_p),
            jnp.stack(mv_p), jnp.stack(conv_s), jnp.stack(sgv_s))
```

```python
import functools
import math

import jax
import jax.numpy as jnp
from jax import lax
from jax.experimental import pallas as pl
from jax.experimental.pallas import tpu as pltpu

F32 = jnp.float32
BF16 = jnp.bfloat16
I32 = jnp.int32
U32 = jnp.uint32

D_MODEL = 1024
N_MEM = 256
CONV_W = 3
SG_GROUPS = 8
SG_GROUP_DIM = D_MODEL // SG_GROUPS
CHUNK = 128
X_HEADS = 4
X_HEAD_DIM = D_MODEL // X_HEADS
N_KEYS = 128
KEY_BITS = N_KEYS.bit_length() - 1
HALF_KEYS = N_KEYS // 2
N_EXPERTS = N_KEYS * N_KEYS
PEER_HEADS = 8
PEER_TOPK = 16
PEER_HALF = 128
N_SEL = PEER_HEADS * PEER_TOPK
EPS = 1e-6

TOKEN_TILE = 256
EXPERT_TOKEN_TILE = 512
EXPERT_TILE = 1024
Q_BUILD_UNROLL = 8
VMEM_LIMIT = 56 * 1024 * 1024

_NT = (((1,), (1,)), ((), ()))

_STAIR = [(a, b) for a in range(PEER_TOPK) for b in range(PEER_TOPK) if (a + 1) * (b + 1) <= PEER_TOPK]


def _params(*sem):
    return pltpu.CompilerParams(dimension_semantics=sem, vmem_limit_bytes=VMEM_LIMIT)


def _const_spec(shape):
    nd = len(shape)
    return pl.BlockSpec(shape, lambda *_: (0,) * nd, pipeline_mode=pl.Buffered(1))


def _rms(x, g):
    return x * lax.rsqrt(jnp.mean(x * x, axis=-1, keepdims=True) + EPS) * g


def _gelu(x):
    c = math.sqrt(2.0 / math.pi)
    return 0.5 * x * (1.0 + jnp.tanh(c * (x + 0.044715 * (x * x * x))))


def _sigmoid(x):
    return 1.0 / (1.0 + jnp.exp(-x))


def _dot(a, b):
    return jnp.dot(a.astype(BF16), b.astype(BF16), preferred_element_type=F32)


def _memkv_kernel(mem_ref, g_ref, wk_ref, wv_ref, k_ref, v_ref):
    m = _rms(mem_ref[...], g_ref[...]).astype(BF16)
    k_ref[...] = jnp.dot(m, wk_ref[...], preferred_element_type=F32)
    v_ref[...] = jnp.dot(m, wv_ref[...], preferred_element_type=F32)


def _mem_kv(mem, g_mem, w_k, w_v):
    t, d = mem.shape
    tm = TOKEN_TILE
    row = pl.BlockSpec((tm, d), lambda i: (i, 0))
    return pl.pallas_call(
        _memkv_kernel,
        grid=(t // tm,),
        in_specs=[row, _const_spec((1, d)), _const_spec((d, d)), _const_spec((d, d))],
        out_specs=[row, row],
        out_shape=[jax.ShapeDtypeStruct((t, d), F32)] * 2,
        compiler_params=_params("arbitrary"),
        name="mem_kv",
    )(mem, g_mem, w_k, w_v)


def _mixer_kernel(is_sample, seq_param, *refs):
    (x_ref, gmix_ref, win_ref, convw_ref, wa_ref, lng_ref, lnb_ref, sgm_ref, sgb_ref, wb_ref,
     wmo_ref) = refs[:11]
    if is_sample:
        p1_ref, p2_ref, out_ref, z_ref, vn_ref = refs[11:]
    else:
        out_ref, ztail_ref, tail_ref = refs[11:]
    d = D_MODEL
    x = x_ref[...]
    tm = x.shape[0]
    h = _rms(x, gmix_ref[...]).astype(BF16)

    def proj(k):
        return jnp.dot(h, win_ref[:, k * d:(k + 1) * d], preferred_element_type=F32)

    z = proj(1) * proj(2)
    row = lax.broadcasted_iota(I32, (tm, 1), 0)
    roll1 = pltpu.roll(z, 1, 0)
    roll2 = pltpu.roll(z, 2, 0)
    if is_sample:
        pos = row % seq_param
        prev1 = jnp.where(pos < 1, p1_ref[...], roll1)
        prev2 = jnp.where(pos < 2, p2_ref[...], roll2)
        z_ref[...] = z
    else:
        @pl.when(pl.program_id(0) % seq_param == 0)
        def _new_sequence():
            tail_ref[...] = jnp.zeros_like(tail_ref)

        t6 = tail_ref[6:7, :]
        t7 = tail_ref[7:8, :]
        prev1 = jnp.where(row < 1, t7, roll1)
        prev2 = jnp.where(row < 1, t6, jnp.where(row < 2, t7, roll2))
        tail_ref[...] = z[tm - 8:, :]
        ztail_ref[...] = z[tm - 8:, :]
    cw = convw_ref[...]
    conv = cw[0:1, :] * prev2 + cw[1:2, :] * prev1 + cw[2:3, :] * z
    o_a = _dot(proj(0) * conv, wa_ref[...])
    m = _sigmoid(proj(5)) * o_a

    v = _gelu(proj(4))
    mu = jnp.mean(v, axis=-1, keepdims=True)
    vc = v - mu
    var = jnp.mean(vc * vc, axis=-1, keepdims=True)
    vn = vc * lax.rsqrt(var + EPS) * lng_ref[...] + lnb_ref[...]
    if is_sample:
        vn_ref[...] = vn
    vnb = vn.astype(BF16)
    gd = SG_GROUP_DIM
    s = jnp.concatenate(
        [jnp.dot(sgm_ref[g], vnb[:, g * gd:(g + 1) * gd], preferred_element_type=F32)
         for g in range(SG_GROUPS)], axis=1) + sgb_ref[...]
    o_b = _dot(_gelu(proj(3)) * s, wb_ref[...])
    m = m + _sigmoid(proj(6)) * o_b

    out_ref[...] = x + _dot(m, wmo_ref[...])


def _mixer(x, w, is_sample, row0, n_rows, seq_len, state=None):
    d = x.shape[1]
    tm = TOKEN_TILE
    n_tiles = n_rows // tm
    tile0 = row0 // tm
    row = pl.BlockSpec((tm, d), lambda i: (i + tile0, 0))
    local = pl.BlockSpec((tm, d), lambda i: (i, 0))
    in_specs = [row, _const_spec((1, d)), _const_spec((d, 7 * d)), _const_spec((CONV_W, d)),
                _const_spec((d, d)), _const_spec((1, d)), _const_spec((1, d)),
                _const_spec((SG_GROUPS, tm, tm)), _const_spec((tm, d)), _const_spec((d, d)),
                _const_spec((d, d))]
    sgm, sgb = (w["sgm_s"], w["sgb_s"]) if is_sample else (w["sgm_p"], w["sgb_p"])
    args = [x, w["g_mix"], w["w_in"], w["conv_w"], w["w_a_out"], w["sg_ln_g"], w["sg_ln_b"],
            sgm, sgb, w["w_b_out"], w["w_mix_o"]]
    if is_sample:
        p2 = jnp.pad(state, ((0, 0), (0, seq_len - 2), (0, 0))).reshape(n_rows, d)
        p1 = jnp.pad(state[:, 1:2], ((0, 0), (0, seq_len - 1), (0, 0))).reshape(n_rows, d)
        in_specs += [local, local]
        args += [p1, p2]
        out_specs = [row, local, local]
        out_shape = [jax.ShapeDtypeStruct(x.shape, F32)] + [jax.ShapeDtypeStruct((n_rows, d), F32)] * 2
        scratch = []
    else:
        out_specs = [row, pl.BlockSpec((8, d), lambda i: (i, 0))]
        out_shape = [jax.ShapeDtypeStruct(x.shape, F32), jax.ShapeDtypeStruct((n_tiles * 8, d), F32)]
        scratch = [pltpu.VMEM((8, d), F32)]
    outs = pl.pallas_call(
        functools.partial(_mixer_kernel, is_sample, seq_len if is_sample else seq_len // tm),
        grid=(n_tiles,),
        in_specs=in_specs,
        out_specs=out_specs,
        out_shape=out_shape,
        scratch_shapes=scratch,
        input_output_aliases={0: 0},
        compiler_params=_params("arbitrary"),
        name="mixer_s" if is_sample else "mixer_p",
    )(*args)
    n_seq = n_rows // seq_len
    if is_sample:
        x_new, z, vn = outs
        return x_new, z.reshape(n_seq, seq_len, d)[:, seq_len - 2:, :], vn
    x_new, ztail = outs
    tps = seq_len // tm
    return x_new, ztail.reshape(n_seq, tps, 8, d)[:, tps - 1, 6:, :], None


def _attn_kernel(ns, tl, x_ref, g_ref, wq_ref, wo_ref, k_ref, v_ref, out_ref):
    x = x_ref[...]
    h = _rms(x, g_ref[...]).astype(BF16)
    q = jnp.dot(h, wq_ref[...], preferred_element_type=F32) * (1.0 / math.sqrt(X_HEAD_DIM))
    e = X_HEAD_DIM
    seqs = []
    for s in range(ns):
        qs = q[s * tl:(s + 1) * tl].astype(BF16)
        kb = k_ref[s].astype(BF16)
        vb = v_ref[s].astype(BF16)
        heads = []
        for hd in range(X_HEADS):
            sc = lax.dot_general(qs[:, hd * e:(hd + 1) * e], kb[:, hd * e:(hd + 1) * e], _NT,
                                 preferred_element_type=F32)
            p = jnp.exp(sc - jnp.max(sc, axis=-1, keepdims=True))
            p = p / jnp.sum(p, axis=-1, keepdims=True)
            heads.append(jnp.dot(p.astype(BF16), vb[:, hd * e:(hd + 1) * e],
                                 preferred_element_type=F32))
        seqs.append(jnp.concatenate(heads, axis=1))
    o = seqs[0] if ns == 1 else jnp.concatenate(seqs, axis=0)
    out_ref[...] = x + _dot(o, wo_ref[...])


def _cross_attn(x, mem_k, mem_v, w, row0, n_rows, seq_len):
    d = x.shape[1]
    if seq_len >= TOKEN_TILE:
        ns, tl = 1, TOKEN_TILE
        tps = seq_len // tl
        kv_map = lambda i: (i // tps, 0, 0)
    else:
        ns, tl = 4, seq_len
        kv_map = lambda i: (i, 0, 0)
    rows = ns * tl
    blk0 = row0 // rows
    row = pl.BlockSpec((rows, d), lambda i: (i + blk0, 0))
    kv = pl.BlockSpec((ns, N_MEM, d), kv_map)
    return pl.pallas_call(
        functools.partial(_attn_kernel, ns, tl),
        grid=(n_rows // rows,),
        in_specs=[row, _const_spec((1, d)), _const_spec((d, d)), _const_spec((d, d)), kv, kv],
        out_specs=row,
        out_shape=jax.ShapeDtypeStruct(x.shape, F32),
        input_output_aliases={0: 0},
        compiler_params=_params("arbitrary"),
        name="xattn",
    )(x, w["g_xattn"], w["w_xq"], w["w_xo"], mem_k, mem_v)


def _topk_rows(s, k):
    n = s.shape[0]
    iota = lax.broadcasted_iota(I32, s.shape, 0).astype(F32)
    vals, idxs = [], []
    for _ in range(k):
        m = jnp.max(s, axis=0, keepdims=True)
        am = jnp.min(jnp.where(s == m, iota, float(n)), axis=0, keepdims=True)
        vals.append(m)
        idxs.append(am)
        s = jnp.where(iota == am, -jnp.inf, s)
    return jnp.concatenate(vals, axis=0), jnp.concatenate(idxs, axis=0)


def _route_kernel(x_ref, g_ref, wpq_ref, keys_ref, h_ref, i1_ref, i2_ref, gate_ref):
    h = _rms(x_ref[...], g_ref[...]).astype(BF16)
    h_ref[...] = h
    q = jnp.dot(h, wpq_ref[...], preferred_element_type=F32)
    tm = q.shape[0]
    k = PEER_TOPK
    n_cand = -(-len(_STAIR) // 8) * 8
    pad = n_cand - len(_STAIR)
    sels, gates = [], []
    for hd in range(PEER_HEADS):
        half = []
        for p in range(2):
            c0 = (hd * 2 + p) * PEER_HALF
            st = lax.dot_general(keys_ref[hd * 2 + p], q[:, c0:c0 + PEER_HALF].astype(BF16), _NT,
                                 preferred_element_type=F32)
            half.append(_topk_rows(st, k))
        (s1, i1), (s2, i2) = half
        cand, cidx = [], []
        for a in range(k):
            nb = sum(1 for (aa, _) in _STAIR if aa == a)
            cand.append(s1[a:a + 1, :] + s2[:nb, :])
            cidx.append(i1[a:a + 1, :] * float(N_KEYS) + i2[:nb, :])
        cand = jnp.concatenate(cand + [jnp.full((pad, tm), -jnp.inf, F32)], axis=0)
        cidx = jnp.concatenate(cidx + [jnp.full((pad, tm), -1.0, F32)], axis=0)
        iota = lax.broadcasted_iota(I32, cand.shape, 0).astype(F32)
        top, sel = [], []
        for _ in range(k):
            m = jnp.max(cand, axis=0, keepdims=True)
            am = jnp.min(jnp.where(cand == m, iota, float(n_cand)), axis=0, keepdims=True)
            hit = iota == am
            top.append(m)
            sel.append(jnp.max(jnp.where(hit, cidx, -1.0), axis=0, keepdims=True))
            cand = jnp.where(hit, -jnp.inf, cand)
        top = jnp.concatenate(top, axis=0)
        e = jnp.exp(top - top[0:1, :])
        gates.append(e / jnp.sum(e, axis=0, keepdims=True))
        sels.append(jnp.concatenate(sel, axis=0))
    sel = jnp.concatenate(sels, axis=0).T.astype(I32)
    i1_ref[...] = sel >> KEY_BITS
    i2_ref[...] = sel & (N_KEYS - 1)
    gate_ref[...] = jnp.concatenate(gates, axis=0).T


def _peer_route(x, w):
    t, d = x.shape
    tm = TOKEN_TILE
    nq = PEER_HEADS * 2 * PEER_HALF
    row = pl.BlockSpec((tm, d), lambda i: (i, 0))
    sel = pl.BlockSpec((tm, N_SEL), lambda i: (i, 0))
    return pl.pallas_call(
        _route_kernel,
        grid=(t // tm,),
        in_specs=[row, _const_spec((1, d)), _const_spec((d, nq)),
                  _const_spec((PEER_HEADS * 2, N_KEYS, PEER_HALF))],
        out_specs=[row, sel, sel, sel],
        out_shape=[jax.ShapeDtypeStruct((t, d), BF16), jax.ShapeDtypeStruct((t, N_SEL), I32),
                   jax.ShapeDtypeStruct((t, N_SEL), I32), jax.ShapeDtypeStruct((t, N_SEL), F32)],
        compiler_params=_params("arbitrary"),
        name="peer_route",
    )(x, w["g_ffn"], w["w_pq"], w["sub_keys"])


def _expert_kernel(final_norm, x_ref, h_ref, i1_ref, i2_ref, gate_ref, ua_ref, ub_ref, va_ref,
                   vb_ref, gf_ref, out_ref, q_ref, acc_ref):
    e = pl.program_id(1)
    tm = x_ref.shape[0]
    rows_per_step = ua_ref.shape[0] // N_KEYS

    @pl.when(e == 0)
    def _build_q():
        acc_ref[...] = jnp.zeros_like(acc_ref)
        sub = lax.broadcasted_iota(I32, (N_KEYS, N_SEL), 0)

        def body(t, carry):
            i1 = i1_ref[pl.ds(t, 1), :]
            i2 = i2_ref[pl.ds(t, 1), :]
            g = gate_ref[pl.ds(t, 1), :]
            one_r = jnp.where(sub == i1, 1.0, 0.0).astype(BF16)
            g_c = jnp.where(sub == i2, g, 0.0).astype(BF16)
            qt = lax.dot_general(one_r, g_c, _NT, preferred_element_type=F32)
            hi = lax.bitcast_convert_type(qt[:HALF_KEYS].astype(BF16).astype(F32), U32)
            lo = lax.bitcast_convert_type(qt[HALF_KEYS:].astype(BF16).astype(F32), U32)
            q_ref[pl.ds(pl.multiple_of(t * HALF_KEYS, HALF_KEYS), HALF_KEYS), :] = hi | (lo >> 16)
            return carry

        lax.fori_loop(0, tm, body, 0, unroll=Q_BUILD_UNROLL)

    h = h_ref[...]
    words = [q_ref[pl.ds(e * rows_per_step + r, tm, stride=HALF_KEYS), :] for r in range(rows_per_step)]
    qa = jnp.concatenate([lax.bitcast_convert_type(wd & jnp.uint32(0xFFFF0000), F32) for wd in words], axis=1)
    qb = jnp.concatenate([lax.bitcast_convert_type(wd << 16, F32) for wd in words], axis=1)
    a = lax.dot_general(h, ua_ref[...], _NT, preferred_element_type=F32)
    b = lax.dot_general(h, ub_ref[...], _NT, preferred_element_type=F32)
    pa = (qa * _gelu(a)).astype(BF16)
    pb = (qb * _gelu(b)).astype(BF16)
    acc_ref[...] += (jnp.dot(pa, va_ref[...], preferred_element_type=F32)
                     + jnp.dot(pb, vb_ref[...], preferred_element_type=F32))

    @pl.when(e == pl.num_programs(1) - 1)
    def _finish():
        y = x_ref[...] + acc_ref[...]
        if final_norm:
            y = _rms(y, gf_ref[...])
        out_ref[...] = y


def _peer_experts(x, h, i1, i2, gate, w, g_final, final_norm):
    t, d = x.shape
    tm, th = EXPERT_TOKEN_TILE, EXPERT_TILE // 2
    n_steps = N_EXPERTS // EXPERT_TILE
    row = pl.BlockSpec((tm, d), lambda i, e: (i, 0), pipeline_mode=pl.Buffered(1))
    sel = pl.BlockSpec((tm, N_SEL), lambda i, e: (i, 0))
    lower = pl.BlockSpec((th, d), lambda i, e: (e, 0))
    upper = pl.BlockSpec((th, d), lambda i, e: (e + n_steps, 0))
    return pl.pallas_call(
        functools.partial(_expert_kernel, final_norm),
        grid=(t // tm, n_steps),
        in_specs=[row, row, sel, sel, sel, lower, upper, lower, upper, _const_spec((1, d))],
        out_specs=pl.BlockSpec((tm, d), lambda i, e: (i, 0)),
        out_shape=jax.ShapeDtypeStruct((t, d), F32),
        scratch_shapes=[pltpu.VMEM((tm * HALF_KEYS, N_KEYS), U32), pltpu.VMEM((tm, d), F32)],
        input_output_aliases={0: 0},
        compiler_params=_params("arbitrary", "arbitrary"),
        name="peer_experts",
    )(x, h, i1, i2, gate, w["peer_u"], w["peer_u"], w["peer_v"], w["peer_v"], g_final)


def _spatial_gate_operands(sg_w, sg_b, c, tm):
    mask = jnp.tril(jnp.ones((c, c), dtype=bool))
    wc = jnp.where(mask, sg_w[:, :c, :c], 0)
    eye = jnp.eye(tm // c, dtype=wc.dtype)
    sgm = jnp.einsum("ab,gts->gatbs", eye, wc).reshape(SG_GROUPS, tm, tm).astype(BF16)
    bias = jnp.repeat(sg_b[:, :c].T, SG_GROUP_DIM, axis=1)
    return sgm, jnp.tile(bias, (tm // c, 1))


def kernel(x_prompt, x_sample, state_conv, cache_mem_k, cache_mem_v, mem_prompt, g_mix, w_in, conv_w, w_a_out, sg_ln_g, sg_ln_b, sg_w, sg_b, w_b_out, w_mix_o, g_xattn, g_mem, w_xq, w_xk, w_xv, w_xo, g_ffn, w_pq, sub_keys, peer_u, peer_v, g_final):
    depth = g_mix.shape[0]
    bp, lp, d = x_prompt.shape
    bs, ls, _ = x_sample.shape
    tp, ts = bp * lp, bs * ls
    x = jnp.concatenate([x_prompt.reshape(tp, d), x_sample.reshape(ts, d)], axis=0)
    mem = mem_prompt.reshape(bp * N_MEM, d)
    gf = g_final.reshape(1, d)
    conv_p, mk_p, mv_p, conv_s, sgv_s = [], [], [], [], []
    for l in range(depth):
        sgm_p, sgb_p = _spatial_gate_operands(sg_w[l], sg_b[l], min(CHUNK, lp), TOKEN_TILE)
        sgm_s, sgb_s = _spatial_gate_operands(sg_w[l], sg_b[l], min(CHUNK, ls), TOKEN_TILE)
        w = dict(
            g_mix=g_mix[l].reshape(1, d), w_in=w_in[l].astype(BF16), conv_w=conv_w[l],
            w_a_out=w_a_out[l].astype(BF16), sg_ln_g=sg_ln_g[l].reshape(1, d),
            sg_ln_b=sg_ln_b[l].reshape(1, d), sgm_p=sgm_p, sgb_p=sgb_p, sgm_s=sgm_s, sgb_s=sgb_s,
            w_b_out=w_b_out[l].astype(BF16), w_mix_o=w_mix_o[l].astype(BF16),
            g_xattn=g_xattn[l].reshape(1, d), w_xq=w_xq[l].astype(BF16), w_xo=w_xo[l].astype(BF16),
            g_ffn=g_ffn[l].reshape(1, d), w_pq=w_pq[l].astype(BF16),
            sub_keys=sub_keys[l].reshape(PEER_HEADS * 2, N_KEYS, PEER_HALF).astype(BF16),
            peer_u=peer_u[l].astype(BF16), peer_v=peer_v[l].astype(BF16))

        mk, mv = _mem_kv(mem, g_mem[l].reshape(1, d), w_xk[l].astype(BF16), w_xv[l].astype(BF16))
        mk_p.append(mk.reshape(bp, N_MEM, X_HEADS, X_HEAD_DIM))
        mv_p.append(mv.reshape(bp, N_MEM, X_HEADS, X_HEAD_DIM))

        x, nb_p, _ = _mixer(x, w, False, 0, tp, lp)
        x, nb_s, vn_s = _mixer(x, w, True, tp, ts, ls, state_conv[l])
        conv_p.append(nb_p)
        conv_s.append(nb_s)
        sgv_s.append(vn_s.reshape(bs, ls, d))
        x = _cross_attn(x, mk.reshape(bp, N_MEM, d), mv.reshape(bp, N_MEM, d), w, 0, tp, lp)
        x = _cross_attn(x, cache_mem_k[l].reshape(bs, N_MEM, d), cache_mem_v[l].reshape(bs, N_MEM, d),
                        w, tp, ts, ls)
        x = _peer_experts(x, *_peer_route(x, w), w, gf, l == depth - 1)

    return (x[:tp].reshape(bp, lp, d), x[tp:].reshape(bs, ls, d), jnp.stack(conv_p), jnp.stack(mk_p),
            jnp.stack(mv_p), jnp.stack(conv_s), jnp.stack(sgv_s))
```

```python
import functools
import math

import jax
import jax.numpy as jnp
from jax import lax
from jax.experimental import pallas as pl
from jax.experimental.pallas import tpu as pltpu

F32 = jnp.float32
BF16 = jnp.bfloat16
I32 = jnp.int32
U32 = jnp.uint32

D_MODEL = 1024
N_MEM = 256
CONV_W = 3
SG_GROUPS = 8
SG_GROUP_DIM = D_MODEL // SG_GROUPS
CHUNK = 128
X_HEADS = 4
X_HEAD_DIM = D_MODEL // X_HEADS
N_KEYS = 128
KEY_BITS = N_KEYS.bit_length() - 1
HALF_KEYS = N_KEYS // 2
N_EXPERTS = N_KEYS * N_KEYS
PEER_HEADS = 8
PEER_TOPK = 16
PEER_HALF = 128
N_SEL = PEER_HEADS * PEER_TOPK
EPS = 1e-6

TOKEN_TILE = 256
EXPERT_TOKEN_TILE = 512
EXPERT_TILE = 1024
Q_BUILD_UNROLL = 16
VMEM_LIMIT = 56 * 1024 * 1024

_NT = (((1,), (1,)), ((), ()))

_STAIR = [(a, b) for a in range(PEER_TOPK) for b in range(PEER_TOPK) if (a + 1) * (b + 1) <= PEER_TOPK]


def _params(*sem):
    return pltpu.CompilerParams(dimension_semantics=sem, vmem_limit_bytes=VMEM_LIMIT)


def _const_spec(shape):
    nd = len(shape)
    return pl.BlockSpec(shape, lambda *_: (0,) * nd, pipeline_mode=pl.Buffered(1))


def _rms(x, g):
    return x * lax.rsqrt(jnp.mean(x * x, axis=-1, keepdims=True) + EPS) * g


def _gelu(x):
    c = math.sqrt(2.0 / math.pi)
    return 0.5 * x * (1.0 + jnp.tanh(c * (x + 0.044715 * (x * x * x))))


def _half_gated_gelu(q_half, x):
    c = math.sqrt(2.0 / math.pi)
    return (q_half * x) * (1.0 + jnp.tanh(x * (c + (c * 0.044715) * (x * x))))


def _sigmoid(x):
    return 1.0 / (1.0 + jnp.exp(-x))


def _dot(a, b):
    return jnp.dot(a.astype(BF16), b.astype(BF16), preferred_element_type=F32)


def _memkv_kernel(mem_ref, g_ref, wk_ref, wv_ref, k_ref, v_ref):
    m = _rms(mem_ref[...], g_ref[...]).astype(BF16)
    k_ref[...] = jnp.dot(m, wk_ref[...], preferred_element_type=F32)
    v_ref[...] = jnp.dot(m, wv_ref[...], preferred_element_type=F32)


def _mem_kv(mem, g_mem, w_k, w_v):
    t, d = mem.shape
    tm = TOKEN_TILE
    row = pl.BlockSpec((tm, d), lambda i: (i, 0))
    return pl.pallas_call(
        _memkv_kernel,
        grid=(t // tm,),
        in_specs=[row, _const_spec((1, d)), _const_spec((d, d)), _const_spec((d, d))],
        out_specs=[row, row],
        out_shape=[jax.ShapeDtypeStruct((t, d), F32)] * 2,
        compiler_params=_params("arbitrary"),
        name="mem_kv",
    )(mem, g_mem, w_k, w_v)


def _mixer_kernel(is_sample, seq_param, *refs):
    (x_ref, gmix_ref, win_ref, convw_ref, wa_ref, lng_ref, lnb_ref, sgm_ref, sgb_ref, wb_ref,
     wmo_ref) = refs[:11]
    if is_sample:
        p1_ref, p2_ref, out_ref, z_ref, vn_ref = refs[11:]
    else:
        out_ref, ztail_ref, tail_ref = refs[11:]
    d = D_MODEL
    x = x_ref[...]
    tm = x.shape[0]
    h = _rms(x, gmix_ref[...]).astype(BF16)

    def proj(k):
        return jnp.dot(h, win_ref[:, k * d:(k + 1) * d], preferred_element_type=F32)

    z = proj(1) * proj(2)
    row = lax.broadcasted_iota(I32, (tm, 1), 0)
    roll1 = pltpu.roll(z, 1, 0)
    roll2 = pltpu.roll(z, 2, 0)
    if is_sample:
        pos = row % seq_param
        prev1 = jnp.where(pos < 1, p1_ref[...], roll1)
        prev2 = jnp.where(pos < 2, p2_ref[...], roll2)
        z_ref[...] = z
    else:
        @pl.when(pl.program_id(0) % seq_param == 0)
        def _new_sequence():
            tail_ref[...] = jnp.zeros_like(tail_ref)

        t6 = tail_ref[6:7, :]
        t7 = tail_ref[7:8, :]
        prev1 = jnp.where(row < 1, t7, roll1)
        prev2 = jnp.where(row < 1, t6, jnp.where(row < 2, t7, roll2))
        tail_ref[...] = z[tm - 8:, :]
        ztail_ref[...] = z[tm - 8:, :]
    cw = convw_ref[...]
    conv = cw[0:1, :] * prev2 + cw[1:2, :] * prev1 + cw[2:3, :] * z
    o_a = _dot(proj(0) * conv, wa_ref[...])
    m = _sigmoid(proj(5)) * o_a

    v = _gelu(proj(4))
    mu = jnp.mean(v, axis=-1, keepdims=True)
    vc = v - mu
    var = jnp.mean(vc * vc, axis=-1, keepdims=True)
    vn = vc * lax.rsqrt(var + EPS) * lng_ref[...] + lnb_ref[...]
    if is_sample:
        vn_ref[...] = vn
    vnb = vn.astype(BF16)
    gd = SG_GROUP_DIM
    s = jnp.concatenate(
        [jnp.dot(sgm_ref[g], vnb[:, g * gd:(g + 1) * gd], preferred_element_type=F32)
         for g in range(SG_GROUPS)], axis=1) + sgb_ref[...]
    o_b = _dot(_gelu(proj(3)) * s, wb_ref[...])
    m = m + _sigmoid(proj(6)) * o_b

    out_ref[...] = x + _dot(m, wmo_ref[...])


def _mixer(x, w, is_sample, row0, n_rows, seq_len, state=None):
    d = x.shape[1]
    tm = TOKEN_TILE
    n_tiles = n_rows // tm
    tile0 = row0 // tm
    row = pl.BlockSpec((tm, d), lambda i: (i + tile0, 0))
    local = pl.BlockSpec((tm, d), lambda i: (i, 0))
    in_specs = [row, _const_spec((1, d)), _const_spec((d, 7 * d)), _const_spec((CONV_W, d)),
                _const_spec((d, d)), _const_spec((1, d)), _const_spec((1, d)),
                _const_spec((SG_GROUPS, tm, tm)), _const_spec((tm, d)), _const_spec((d, d)),
                _const_spec((d, d))]
    sgm, sgb = (w["sgm_s"], w["sgb_s"]) if is_sample else (w["sgm_p"], w["sgb_p"])
    args = [x, w["g_mix"], w["w_in"], w["conv_w"], w["w_a_out"], w["sg_ln_g"], w["sg_ln_b"],
            sgm, sgb, w["w_b_out"], w["w_mix_o"]]
    if is_sample:
        p2 = jnp.pad(state, ((0, 0), (0, seq_len - 2), (0, 0))).reshape(n_rows, d)
        p1 = jnp.pad(state[:, 1:2], ((0, 0), (0, seq_len - 1), (0, 0))).reshape(n_rows, d)
        in_specs += [local, local]
        args += [p1, p2]
        out_specs = [row, local, local]
        out_shape = [jax.ShapeDtypeStruct(x.shape, F32)] + [jax.ShapeDtypeStruct((n_rows, d), F32)] * 2
        scratch = []
    else:
        out_specs = [row, pl.BlockSpec((8, d), lambda i: (i, 0))]
        out_shape = [jax.ShapeDtypeStruct(x.shape, F32), jax.ShapeDtypeStruct((n_tiles * 8, d), F32)]
        scratch = [pltpu.VMEM((8, d), F32)]
    outs = pl.pallas_call(
        functools.partial(_mixer_kernel, is_sample, seq_len if is_sample else seq_len // tm),
        grid=(n_tiles,),
        in_specs=in_specs,
        out_specs=out_specs,
        out_shape=out_shape,
        scratch_shapes=scratch,
        input_output_aliases={0: 0},
        compiler_params=_params("arbitrary"),
        name="mixer_s" if is_sample else "mixer_p",
    )(*args)
    n_seq = n_rows // seq_len
    if is_sample:
        x_new, z, vn = outs
        return x_new, z.reshape(n_seq, seq_len, d)[:, seq_len - 2:, :], vn
    x_new, ztail = outs
    tps = seq_len // tm
    return x_new, ztail.reshape(n_seq, tps, 8, d)[:, tps - 1, 6:, :], None


def _attn_kernel(ns, tl, x_ref, g_ref, wq_ref, wo_ref, k_ref, v_ref, out_ref):
    x = x_ref[...]
    h = _rms(x, g_ref[...]).astype(BF16)
    q = jnp.dot(h, wq_ref[...], preferred_element_type=F32) * (1.0 / math.sqrt(X_HEAD_DIM))
    e = X_HEAD_DIM
    seqs = []
    for s in range(ns):
        qs = q[s * tl:(s + 1) * tl].astype(BF16)
        kb = k_ref[s].astype(BF16)
        vb = v_ref[s].astype(BF16)
        heads = []
        for hd in range(X_HEADS):
            sc = lax.dot_general(qs[:, hd * e:(hd + 1) * e], kb[:, hd * e:(hd + 1) * e], _NT,
                                 preferred_element_type=F32)
            p = jnp.exp(sc - jnp.max(sc, axis=-1, keepdims=True))
            p = p / jnp.sum(p, axis=-1, keepdims=True)
            heads.append(jnp.dot(p.astype(BF16), vb[:, hd * e:(hd + 1) * e],
                                 preferred_element_type=F32))
        seqs.append(jnp.concatenate(heads, axis=1))
    o = seqs[0] if ns == 1 else jnp.concatenate(seqs, axis=0)
    out_ref[...] = x + _dot(o, wo_ref[...])


def _cross_attn(x, mem_k, mem_v, w, row0, n_rows, seq_len):
    d = x.shape[1]
    assert seq_len % TOKEN_TILE == 0
    ns, tl = 1, TOKEN_TILE
    tps = seq_len // tl
    kv_map = lambda i: (i // tps, 0, 0)
    rows = ns * tl
    blk0 = row0 // rows
    row = pl.BlockSpec((rows, d), lambda i: (i + blk0, 0))
    kv = pl.BlockSpec((ns, N_MEM, d), kv_map)
    return pl.pallas_call(
        functools.partial(_attn_kernel, ns, tl),
        grid=(n_rows // rows,),
        in_specs=[row, _const_spec((1, d)), _const_spec((d, d)), _const_spec((d, d)), kv, kv],
        out_specs=row,
        out_shape=jax.ShapeDtypeStruct(x.shape, F32),
        input_output_aliases={0: 0},
        compiler_params=_params("arbitrary"),
        name="xattn",
    )(x, w["g_xattn"], w["w_xq"], w["w_xo"], mem_k, mem_v)


def _attn_cached_kernel(layer, ns, tl, x_ref, g_ref, wq_ref, wo_ref, k_hbm, v_hbm, out_ref,
                        q_ref, o_ref, kbuf, vbuf, ksem, vsem):
    i = pl.program_id(0)
    n_steps = pl.num_programs(0)
    e = X_HEAD_DIM

    def copies(step, slot):
        out = []
        for s in range(ns):
            b = step * ns + s
            for hd in range(X_HEADS):
                out.append(pltpu.make_async_copy(k_hbm.at[layer, b, :, hd, :], kbuf.at[slot, s, hd],
                                                 ksem.at[slot]))
                out.append(pltpu.make_async_copy(v_hbm.at[layer, b, :, hd, :], vbuf.at[slot, s, hd],
                                                 vsem.at[slot]))
        return out

    @pl.when(i == 0)
    def _first():
        for c in copies(0, 0):
            c.start()
        h = _rms(x_ref[...], g_ref[...]).astype(BF16)
        q_ref[...] = jnp.dot(h, wq_ref[...], preferred_element_type=F32) * (1.0 / math.sqrt(e))

    slot = i % 2

    @pl.when(i + 1 < n_steps)
    def _prefetch():
        for c in copies(i + 1, 1 - slot):
            c.start()

    for c in copies(i, slot):
        c.wait()

    for s in range(ns):
        r0 = pl.multiple_of((i * ns + s) * tl, tl)
        qs = q_ref[pl.ds(r0, tl), :].astype(BF16)
        heads = []
        for hd in range(X_HEADS):
            kh = kbuf[slot, s, hd].astype(BF16)
            vh = vbuf[slot, s, hd].astype(BF16)
            sc = lax.dot_general(qs[:, hd * e:(hd + 1) * e], kh, _NT, preferred_element_type=F32)
            p = jnp.exp(sc - jnp.max(sc, axis=-1, keepdims=True))
            p = p / jnp.sum(p, axis=-1, keepdims=True)
            heads.append(jnp.dot(p.astype(BF16), vh, preferred_element_type=F32))
        o_ref[pl.ds(r0, tl), :] = jnp.concatenate(heads, axis=1)

    @pl.when(i == n_steps - 1)
    def _project_output():
        out_ref[...] = x_ref[...] + _dot(o_ref[...], wo_ref[...])


def _cross_attn_cached(x, cache_k, cache_v, layer, w, row0, n_rows, seq_len):
    d = x.shape[1]
    ns = 4
    rows = pl.BlockSpec((n_rows, d), lambda i: (row0 // n_rows, 0), pipeline_mode=pl.Buffered(1))
    hbm = pl.BlockSpec(memory_space=pl.ANY)
    buf = pltpu.VMEM((2, ns, X_HEADS, N_MEM, X_HEAD_DIM), F32)
    return pl.pallas_call(
        functools.partial(_attn_cached_kernel, layer, ns, seq_len),
        grid=(n_rows // (ns * seq_len),),
        in_specs=[rows, _const_spec((1, d)), _const_spec((d, d)), _const_spec((d, d)), hbm, hbm],
        out_specs=pl.BlockSpec((n_rows, d), lambda i: (row0 // n_rows, 0)),
        out_shape=jax.ShapeDtypeStruct(x.shape, F32),
        scratch_shapes=[pltpu.VMEM((n_rows, d), F32), pltpu.VMEM((n_rows, d), F32), buf, buf,
                        pltpu.SemaphoreType.DMA((2,)), pltpu.SemaphoreType.DMA((2,))],
        input_output_aliases={0: 0},
        compiler_params=_params("arbitrary"),
        name="xattn_cached",
    )(x, w["g_xattn"], w["w_xq"], w["w_xo"], cache_k, cache_v)


def _topk_rows(s, k):
    n = s.shape[0]
    iota = lax.broadcasted_iota(I32, s.shape, 0).astype(F32)
    vals, idxs = [], []
    for _ in range(k):
        m = jnp.max(s, axis=0, keepdims=True)
        am = jnp.min(jnp.where(s == m, iota, float(n)), axis=0, keepdims=True)
        vals.append(m)
        idxs.append(am)
        s = jnp.where(iota == am, -jnp.inf, s)
    return jnp.concatenate(vals, axis=0), jnp.concatenate(idxs, axis=0)


def _route_kernel(x_ref, g_ref, wpq_ref, keys_ref, h_ref, i1_ref, i2_ref, gate_ref):
    h = _rms(x_ref[...], g_ref[...]).astype(BF16)
    h_ref[...] = h
    q = jnp.dot(h, wpq_ref[...], preferred_element_type=F32)
    tm = q.shape[0]
    k = PEER_TOPK
    n_cand = -(-len(_STAIR) // 8) * 8
    pad = n_cand - len(_STAIR)
    sels, gates = [], []
    for hd in range(PEER_HEADS):
        half = []
        for p in range(2):
            c0 = (hd * 2 + p) * PEER_HALF
            st = lax.dot_general(keys_ref[hd * 2 + p], q[:, c0:c0 + PEER_HALF].astype(BF16), _NT,
                                 preferred_element_type=F32)
            half.append(_topk_rows(st, k))
        (s1, i1), (s2, i2) = half
        cand, cidx = [], []
        for a in range(k):
            nb = sum(1 for (aa, _) in _STAIR if aa == a)
            cand.append(s1[a:a + 1, :] + s2[:nb, :])
            cidx.append(i1[a:a + 1, :] * float(N_KEYS) + i2[:nb, :])
        cand = jnp.concatenate(cand + [jnp.full((pad, tm), -jnp.inf, F32)], axis=0)
        cidx = jnp.concatenate(cidx + [jnp.full((pad, tm), -1.0, F32)], axis=0)
        iota = lax.broadcasted_iota(I32, cand.shape, 0).astype(F32)
        top, sel = [], []
        for _ in range(k):
            m = jnp.max(cand, axis=0, keepdims=True)
            am = jnp.min(jnp.where(cand == m, iota, float(n_cand)), axis=0, keepdims=True)
            hit = iota == am
            top.append(m)
            sel.append(jnp.max(jnp.where(hit, cidx, -1.0), axis=0, keepdims=True))
            cand = jnp.where(hit, -jnp.inf, cand)
        top = jnp.concatenate(top, axis=0)
        e = jnp.exp(top - top[0:1, :])
        gates.append(e / jnp.sum(e, axis=0, keepdims=True))
        sels.append(jnp.concatenate(sel, axis=0))
    sel = jnp.concatenate(sels, axis=0).T.astype(I32)
    i1_ref[...] = sel >> KEY_BITS
    i2_ref[...] = sel & (N_KEYS - 1)
    gate_ref[...] = jnp.concatenate(gates, axis=0).T


def _peer_route(x, w):
    t, d = x.shape
    tm = TOKEN_TILE
    nq = PEER_HEADS * 2 * PEER_HALF
    row = pl.BlockSpec((tm, d), lambda i: (i, 0))
    sel = pl.BlockSpec((tm, N_SEL), lambda i: (i, 0))
    return pl.pallas_call(
        _route_kernel,
        grid=(t // tm,),
        in_specs=[row, _const_spec((1, d)), _const_spec((d, nq)),
                  _const_spec((PEER_HEADS * 2, N_KEYS, PEER_HALF))],
        out_specs=[row, sel, sel, sel],
        out_shape=[jax.ShapeDtypeStruct((t, d), BF16), jax.ShapeDtypeStruct((t, N_SEL), I32),
                   jax.ShapeDtypeStruct((t, N_SEL), I32), jax.ShapeDtypeStruct((t, N_SEL), F32)],
        compiler_params=_params("arbitrary"),
        name="peer_route",
    )(x, w["g_ffn"], w["w_pq"], w["sub_keys"])


def _expert_kernel(final_norm, n_blocks, x_ref, h_ref, i1_ref, i2_ref, gate_ref, ua_ref, ub_ref, va_ref,
                   vb_ref, gf_ref, out_ref, q_ref, acc_ref, a0_ref, a1_ref):
    e = pl.program_id(1)
    tm = x_ref.shape[0]
    th = ua_ref.shape[0]
    rows_per_step = th // N_KEYS
    a_refs = (a0_ref, a1_ref)

    def scores(a_ref):
        h = h_ref[...]
        a_ref[:, :th] = lax.dot_general(h, ua_ref[...], _NT, preferred_element_type=F32)
        a_ref[:, th:] = lax.dot_general(h, ub_ref[...], _NT, preferred_element_type=F32)

    def apply(a_ref):
        r0 = (e - 1) * rows_per_step
        words = [q_ref[pl.ds(r0 + r, tm, stride=HALF_KEYS), :] for r in range(rows_per_step)]
        qa = jnp.concatenate(
            [lax.bitcast_convert_type(wd & jnp.uint32(0xFFFF0000), F32) for wd in words], axis=1)
        qb = jnp.concatenate([lax.bitcast_convert_type(wd << 16, F32) for wd in words], axis=1)
        pa = _half_gated_gelu(qa, a_ref[:, :th]).astype(BF16)
        pb = _half_gated_gelu(qb, a_ref[:, th:]).astype(BF16)
        acc_ref[...] += (jnp.dot(pa, va_ref[...], preferred_element_type=F32)
                         + jnp.dot(pb, vb_ref[...], preferred_element_type=F32))

    @pl.when(e == 0)
    def _build_q():
        acc_ref[...] = jnp.zeros_like(acc_ref)
        sub = lax.broadcasted_iota(I32, (N_KEYS, N_SEL), 0)

        def body(t, carry):
            i1 = i1_ref[pl.ds(t, 1), :]
            i2 = i2_ref[pl.ds(t, 1), :]
            g = 0.5 * gate_ref[pl.ds(t, 1), :]
            one_r = jnp.where(sub == i1, 1.0, 0.0).astype(BF16)
            g_c = jnp.where(sub == i2, g, 0.0).astype(BF16)
            qt = lax.dot_general(one_r, g_c, _NT, preferred_element_type=F32)
            hi = lax.bitcast_convert_type(qt[:HALF_KEYS].astype(BF16).astype(F32), U32)
            lo = lax.bitcast_convert_type(qt[HALF_KEYS:].astype(BF16).astype(F32), U32)
            q_ref[pl.ds(pl.multiple_of(t * HALF_KEYS, HALF_KEYS), HALF_KEYS), :] = hi | (lo >> 16)
            return carry

        lax.fori_loop(0, tm, body, 0, unroll=Q_BUILD_UNROLL)
        scores(a_refs[0])

    def middle(parity):
        scores(a_refs[parity])
        apply(a_refs[1 - parity])

    @pl.when((e > 0) & (e < n_blocks) & (e % 2 == 0))
    def _even():
        middle(0)

    @pl.when((e > 0) & (e < n_blocks) & (e % 2 == 1))
    def _odd():
        middle(1)

    @pl.when(e == n_blocks)
    def _finish():
        apply(a_refs[(n_blocks - 1) % 2])
        y = x_ref[...] + acc_ref[...]
        if final_norm:
            y = _rms(y, gf_ref[...])
        out_ref[...] = y


def _peer_experts(x, h, i1, i2, gate, w, g_final, final_norm):
    t, d = x.shape
    tm, th = EXPERT_TOKEN_TILE, EXPERT_TILE // 2
    n_steps = N_EXPERTS // EXPERT_TILE
    row = pl.BlockSpec((tm, d), lambda i, e: (i, 0), pipeline_mode=pl.Buffered(1))
    sel = pl.BlockSpec((tm, N_SEL), lambda i, e: (i, 0))
    u_blk = lambda e: jnp.minimum(e, n_steps - 1)
    v_blk = lambda e: jnp.maximum(e - 1, 0)
    u_lo = pl.BlockSpec((th, d), lambda i, e: (u_blk(e), 0))
    u_hi = pl.BlockSpec((th, d), lambda i, e: (u_blk(e) + n_steps, 0))
    v_lo = pl.BlockSpec((th, d), lambda i, e: (v_blk(e), 0))
    v_hi = pl.BlockSpec((th, d), lambda i, e: (v_blk(e) + n_steps, 0))
    return pl.pallas_call(
        functools.partial(_expert_kernel, final_norm, n_steps),
        grid=(t // tm, n_steps + 1),
        in_specs=[row, row, sel, sel, sel, u_lo, u_hi, v_lo, v_hi, _const_spec((1, d))],
        out_specs=pl.BlockSpec((tm, d), lambda i, e: (i, 0)),
        out_shape=jax.ShapeDtypeStruct((t, d), F32),
        scratch_shapes=[pltpu.VMEM((tm * HALF_KEYS, N_KEYS), U32), pltpu.VMEM((tm, d), F32),
                        pltpu.VMEM((tm, EXPERT_TILE), F32), pltpu.VMEM((tm, EXPERT_TILE), F32)],
        input_output_aliases={0: 0},
        compiler_params=_params("arbitrary", "arbitrary"),
        name="peer_experts",
    )(x, h, i1, i2, gate, w["peer_u"], w["peer_u"], w["peer_v"], w["peer_v"], g_final)


def _spatial_gate_operands(sg_w, sg_b, c, tm):
    mask = jnp.tril(jnp.ones((c, c), dtype=bool))
    wc = jnp.where(mask, sg_w[:, :c, :c], 0)
    eye = jnp.eye(tm // c, dtype=wc.dtype)
    sgm = jnp.einsum("ab,gts->gatbs", eye, wc).reshape(SG_GROUPS, tm, tm).astype(BF16)
    bias = jnp.repeat(sg_b[:, :c].T, SG_GROUP_DIM, axis=1)
    return sgm, jnp.tile(bias, (tm // c, 1))


def kernel(x_prompt, x_sample, state_conv, cache_mem_k, cache_mem_v, mem_prompt, g_mix, w_in, conv_w, w_a_out, sg_ln_g, sg_ln_b, sg_w, sg_b, w_b_out, w_mix_o, g_xattn, g_mem, w_xq, w_xk, w_xv, w_xo, g_ffn, w_pq, sub_keys, peer_u, peer_v, g_final):
    depth = g_mix.shape[0]
    bp, lp, d = x_prompt.shape
    bs, ls, _ = x_sample.shape
    tp, ts = bp * lp, bs * ls
    x = jnp.concatenate([x_prompt.reshape(tp, d), x_sample.reshape(ts, d)], axis=0)
    mem = mem_prompt.reshape(bp * N_MEM, d)
    gf = g_final.reshape(1, d)
    conv_p, mk_p, mv_p, conv_s, sgv_s = [], [], [], [], []
    for l in range(depth):
        sgm_p, sgb_p = _spatial_gate_operands(sg_w[l], sg_b[l], min(CHUNK, lp), TOKEN_TILE)
        sgm_s, sgb_s = _spatial_gate_operands(sg_w[l], sg_b[l], min(CHUNK, ls), TOKEN_TILE)
        w = dict(
            g_mix=g_mix[l].reshape(1, d), w_in=w_in[l].astype(BF16), conv_w=conv_w[l],
            w_a_out=w_a_out[l].astype(BF16), sg_ln_g=sg_ln_g[l].reshape(1, d),
            sg_ln_b=sg_ln_b[l].reshape(1, d), sgm_p=sgm_p, sgb_p=sgb_p, sgm_s=sgm_s, sgb_s=sgb_s,
            w_b_out=w_b_out[l].astype(BF16), w_mix_o=w_mix_o[l].astype(BF16),
            g_xattn=g_xattn[l].reshape(1, d), w_xq=w_xq[l].astype(BF16), w_xo=w_xo[l].astype(BF16),
            g_ffn=g_ffn[l].reshape(1, d), w_pq=w_pq[l].astype(BF16),
            sub_keys=sub_keys[l].reshape(PEER_HEADS * 2, N_KEYS, PEER_HALF).astype(BF16),
            peer_u=peer_u[l].astype(BF16), peer_v=peer_v[l].astype(BF16))

        mk, mv = _mem_kv(mem, g_mem[l].reshape(1, d), w_xk[l].astype(BF16), w_xv[l].astype(BF16))
        mk_p.append(mk.reshape(bp, N_MEM, X_HEADS, X_HEAD_DIM))
        mv_p.append(mv.reshape(bp, N_MEM, X_HEADS, X_HEAD_DIM))

        x, nb_p, _ = _mixer(x, w, False, 0, tp, lp)
        x, nb_s, vn_s = _mixer(x, w, True, tp, ts, ls, state_conv[l])
        conv_p.append(nb_p)
        conv_s.append(nb_s)
        sgv_s.append(vn_s.reshape(bs, ls, d))
        x = _cross_attn(x, mk.reshape(bp, N_MEM, d), mv.reshape(bp, N_MEM, d), w, 0, tp, lp)
        x = _cross_attn_cached(x, cache_mem_k, cache_mem_v, l, w, tp, ts, ls)
        x = _peer_experts(x, *_peer_route(x, w), w, gf, l == depth - 1)

    return (x[:tp].reshape(bp, lp, d), x[tp:].reshape(bs, ls, d), jnp.stack(conv_p), jnp.stack(mk_p),
            jnp.stack(mv_p), jnp.stack(conv_s), jnp.stack(sgv_s))
```

```python
import functools
import math

import jax
import jax.numpy as jnp
from jax import lax
from jax.experimental import pallas as pl
from jax.experimental.pallas import tpu as pltpu

F32 = jnp.float32
BF16 = jnp.bfloat16
I32 = jnp.int32
U32 = jnp.uint32

D_MODEL = 1024
N_MEM = 256
CONV_W = 3
SG_GROUPS = 8
SG_GROUP_DIM = D_MODEL // SG_GROUPS
CHUNK = 128
X_HEADS = 4
X_HEAD_DIM = D_MODEL // X_HEADS
N_KEYS = 128
KEY_BITS = N_KEYS.bit_length() - 1
HALF_KEYS = N_KEYS // 2
N_EXPERTS = N_KEYS * N_KEYS
PEER_HEADS = 8
PEER_TOPK = 16
PEER_HALF = 128
N_SEL = PEER_HEADS * PEER_TOPK
EPS = 1e-6

TOKEN_TILE = 256
EXPERT_TOKEN_TILE = 512
EXPERT_TILE = 1024
Q_BUILD_UNROLL = 32
VMEM_LIMIT = 56 * 1024 * 1024
MXU_DEPTH = 256

_NT = (((1,), (1,)), ((), ()))

_STAIR = [(a, b) for a in range(PEER_TOPK) for b in range(PEER_TOPK) if (a + 1) * (b + 1) <= PEER_TOPK]


def _params(*sem):
    return pltpu.CompilerParams(dimension_semantics=sem, vmem_limit_bytes=VMEM_LIMIT)


def _const_spec(shape):
    nd = len(shape)
    return pl.BlockSpec(shape, lambda *_: (0,) * nd, pipeline_mode=pl.Buffered(1))


def _rms(x, g):
    return x * lax.rsqrt(jnp.mean(x * x, axis=-1, keepdims=True) + EPS) * g


def _gelu(x):
    c = math.sqrt(2.0 / math.pi)
    return 0.5 * x * (1.0 + jnp.tanh(c * (x + 0.044715 * (x * x * x))))


def _half_gated_gelu(q_half, x):
    c = math.sqrt(2.0 / math.pi)
    return (q_half * x) * (1.0 + jnp.tanh(x * (c + (c * 0.044715) * (x * x))))


def _sigmoid(x):
    return 1.0 / (1.0 + jnp.exp(-x))


def _dot(a, b):
    return jnp.dot(a.astype(BF16), b.astype(BF16), preferred_element_type=F32)


def _memkv_kernel(mem_ref, g_ref, wk_ref, wv_ref, k_ref, v_ref):
    m = _rms(mem_ref[...], g_ref[...]).astype(BF16)
    k_ref[...] = jnp.dot(m, wk_ref[...], preferred_element_type=F32)
    v_ref[...] = jnp.dot(m, wv_ref[...], preferred_element_type=F32)


def _mem_kv(mem, g_mem, w_k, w_v):
    t, d = mem.shape
    tm = TOKEN_TILE
    row = pl.BlockSpec((tm, d), lambda i: (i, 0))
    return pl.pallas_call(
        _memkv_kernel,
        grid=(t // tm,),
        in_specs=[row, _const_spec((1, d)), _const_spec((d, d)), _const_spec((d, d))],
        out_specs=[row, row],
        out_shape=[jax.ShapeDtypeStruct((t, d), F32)] * 2,
        compiler_params=_params("arbitrary"),
        name="mem_kv",
    )(mem, g_mem, w_k, w_v)


def _mixer_kernel(is_sample, seq_param, *refs):
    (x_ref, gmix_ref, win_ref, convw_ref, wa_ref, lng_ref, lnb_ref, sgm_ref, sgb_ref, wb_ref,
     wmo_ref) = refs[:11]
    if is_sample:
        p1_ref, p2_ref = refs[11:13]
        out_ref, z_ref, vn_ref = refs[-3:]
    else:
        out_ref, ztail_ref, tail_ref = refs[-3:]
    d = D_MODEL
    x = x_ref[...]
    tm = x.shape[0]
    h = _rms(x, gmix_ref[...]).astype(BF16)

    def proj(k):
        return jnp.dot(h, win_ref[:, k * d:(k + 1) * d], preferred_element_type=F32)

    z = proj(1) * proj(2)
    row = lax.broadcasted_iota(I32, (tm, 1), 0)
    roll1 = pltpu.roll(z, 1, 0)
    roll2 = pltpu.roll(z, 2, 0)
    if is_sample:
        pos = row % seq_param
        prev1 = jnp.where(pos < 1, p1_ref[...], roll1)
        prev2 = jnp.where(pos < 2, p2_ref[...], roll2)
        z_ref[...] = z
    else:
        @pl.when(pl.program_id(0) % seq_param == 0)
        def _new_sequence():
            tail_ref[...] = jnp.zeros_like(tail_ref)

        t6 = tail_ref[6:7, :]
        t7 = tail_ref[7:8, :]
        prev1 = jnp.where(row < 1, t7, roll1)
        prev2 = jnp.where(row < 1, t6, jnp.where(row < 2, t7, roll2))
        tail_ref[...] = z[tm - 8:, :]
        ztail_ref[...] = z[tm - 8:, :]
    cw = convw_ref[...]
    conv = cw[0:1, :] * prev2 + cw[1:2, :] * prev1 + cw[2:3, :] * z
    o_a = _dot(proj(0) * conv, wa_ref[...])
    m = _sigmoid(proj(5)) * o_a

    v = _gelu(proj(4))
    mu = jnp.mean(v, axis=-1, keepdims=True)
    vc = v - mu
    var = jnp.mean(vc * vc, axis=-1, keepdims=True)
    vn = vc * lax.rsqrt(var + EPS) * lng_ref[...] + lnb_ref[...]
    if is_sample:
        vn_ref[...] = vn
    vnb = vn.astype(BF16)
    gd = SG_GROUP_DIM
    s = jnp.concatenate(
        [jnp.dot(sgm_ref[g], vnb[:, g * gd:(g + 1) * gd], preferred_element_type=F32)
         for g in range(SG_GROUPS)], axis=1) + sgb_ref[...]
    o_b = _dot(_gelu(proj(3)) * s, wb_ref[...])
    m = m + _sigmoid(proj(6)) * o_b

    out_ref[...] = x + _dot(m, wmo_ref[...])


def _mixer(stream, w, is_sample, row0, n_rows, seq_len, state=None, x_src=None):
    d = D_MODEL
    tm = TOKEN_TILE
    n_tiles = n_rows // tm
    tile0 = row0 // tm
    row = pl.BlockSpec((tm, d), lambda i: (i + tile0, 0))
    local = pl.BlockSpec((tm, d), lambda i: (i, 0))
    x, x_spec = (stream, row) if x_src is None else (x_src, local)
    stream_shape = stream.shape
    in_specs = [x_spec, _const_spec((1, d)), _const_spec((d, 7 * d)), _const_spec((CONV_W, d)),
                _const_spec((d, d)), _const_spec((1, d)), _const_spec((1, d)),
                _const_spec((SG_GROUPS, tm, tm)), _const_spec((tm, d)), _const_spec((d, d)),
                _const_spec((d, d))]
    sgm, sgb = (w["sgm_s"], w["sgb_s"]) if is_sample else (w["sgm_p"], w["sgb_p"])
    args = [x, w["g_mix"], w["w_in"], w["conv_w"], w["w_a_out"], w["sg_ln_g"], w["sg_ln_b"],
            sgm, sgb, w["w_b_out"], w["w_mix_o"]]
    if is_sample:
        p2 = jnp.pad(state, ((0, 0), (0, seq_len - 2), (0, 0))).reshape(n_rows, d)
        p1 = jnp.pad(state[:, 1:2], ((0, 0), (0, seq_len - 1), (0, 0))).reshape(n_rows, d)
        in_specs += [local, local]
        args += [p1, p2]
        out_specs = [row, local, local]
        out_shape = [jax.ShapeDtypeStruct(stream_shape, F32)] + [jax.ShapeDtypeStruct((n_rows, d), F32)] * 2
        scratch = []
    else:
        out_specs = [row, pl.BlockSpec((8, d), lambda i: (i, 0))]
        out_shape = [jax.ShapeDtypeStruct(stream_shape, F32), jax.ShapeDtypeStruct((n_tiles * 8, d), F32)]
        scratch = [pltpu.VMEM((8, d), F32)]
    if x_src is None:
        aliases = {0: 0}
    else:
        aliases = {len(args): 0}
        in_specs.append(pl.BlockSpec(memory_space=pl.ANY))
        args.append(stream)
    outs = pl.pallas_call(
        functools.partial(_mixer_kernel, is_sample, seq_len if is_sample else seq_len // tm),
        grid=(n_tiles,),
        in_specs=in_specs,
        out_specs=out_specs,
        out_shape=out_shape,
        scratch_shapes=scratch,
        input_output_aliases=aliases,
        compiler_params=_params("arbitrary"),
        name="mixer_s" if is_sample else "mixer_p",
    )(*args)
    n_seq = n_rows // seq_len
    if is_sample:
        x_new, z, vn = outs
        return x_new, z.reshape(n_seq, seq_len, d)[:, seq_len - 2:, :], vn
    x_new, ztail = outs
    tps = seq_len // tm
    return x_new, ztail.reshape(n_seq, tps, 8, d)[:, tps - 1, 6:, :], None


def _attn_kernel(ns, tl, x_ref, g_ref, wq_ref, wo_ref, k_ref, v_ref, out_ref):
    x = x_ref[...]
    h = _rms(x, g_ref[...]).astype(BF16)
    q = jnp.dot(h, wq_ref[...], preferred_element_type=F32) * (1.0 / math.sqrt(X_HEAD_DIM))
    e = X_HEAD_DIM
    seqs = []
    for s in range(ns):
        qs = q[s * tl:(s + 1) * tl].astype(BF16)
        kb = k_ref[s].astype(BF16)
        vb = v_ref[s].astype(BF16)
        heads = []
        for hd in range(X_HEADS):
            sc = lax.dot_general(qs[:, hd * e:(hd + 1) * e], kb[:, hd * e:(hd + 1) * e], _NT,
                                 preferred_element_type=F32)
            p = jnp.exp(sc - jnp.max(sc, axis=-1, keepdims=True))
            p = p / jnp.sum(p, axis=-1, keepdims=True)
            heads.append(jnp.dot(p.astype(BF16), vb[:, hd * e:(hd + 1) * e],
                                 preferred_element_type=F32))
        seqs.append(jnp.concatenate(heads, axis=1))
    o = seqs[0] if ns == 1 else jnp.concatenate(seqs, axis=0)
    out_ref[...] = x + _dot(o, wo_ref[...])


def _cross_attn(x, mem_k, mem_v, w, row0, n_rows, seq_len):
    d = x.shape[1]
    assert seq_len % TOKEN_TILE == 0
    ns, tl = 1, TOKEN_TILE
    tps = seq_len // tl
    kv_map = lambda i: (i // tps, 0, 0)
    rows = ns * tl
    blk0 = row0 // rows
    row = pl.BlockSpec((rows, d), lambda i: (i + blk0, 0))
    kv = pl.BlockSpec((ns, N_MEM, d), kv_map)
    return pl.pallas_call(
        functools.partial(_attn_kernel, ns, tl),
        grid=(n_rows // rows,),
        in_specs=[row, _const_spec((1, d)), _const_spec((d, d)), _const_spec((d, d)), kv, kv],
        out_specs=row,
        out_shape=jax.ShapeDtypeStruct(x.shape, F32),
        input_output_aliases={0: 0},
        compiler_params=_params("arbitrary"),
        name="xattn",
    )(x, w["g_xattn"], w["w_xq"], w["w_xo"], mem_k, mem_v)


def _attn_cached_kernel(layer, ns, tl, x_ref, g_ref, wq_ref, wo_ref, k_hbm, v_hbm, out_ref,
                        q_ref, o_ref, kbuf, vbuf, ksem, vsem):
    i = pl.program_id(0)
    n_steps = pl.num_programs(0)
    e = X_HEAD_DIM

    def copies(step, slot):
        out = []
        for s in range(ns):
            b = step * ns + s
            for hd in range(X_HEADS):
                out.append(pltpu.make_async_copy(k_hbm.at[layer, b, :, hd, :], kbuf.at[slot, s, hd],
                                                 ksem.at[slot]))
                out.append(pltpu.make_async_copy(v_hbm.at[layer, b, :, hd, :], vbuf.at[slot, s, hd],
                                                 vsem.at[slot]))
        return out

    @pl.when(i == 0)
    def _first():
        for c in copies(0, 0):
            c.start()
        h = _rms(x_ref[...], g_ref[...]).astype(BF16)
        q_ref[...] = jnp.dot(h, wq_ref[...], preferred_element_type=F32) * (1.0 / math.sqrt(e))

    slot = i % 2

    @pl.when(i + 1 < n_steps)
    def _prefetch():
        for c in copies(i + 1, 1 - slot):
            c.start()

    for c in copies(i, slot):
        c.wait()

    for s in range(ns):
        r0 = pl.multiple_of((i * ns + s) * tl, tl)
        qs = q_ref[pl.ds(r0, tl), :].astype(BF16)
        heads = []
        for hd in range(X_HEADS):
            kh = kbuf[slot, s, hd].astype(BF16)
            vh = vbuf[slot, s, hd].astype(BF16)
            sc = lax.dot_general(qs[:, hd * e:(hd + 1) * e], kh, _NT, preferred_element_type=F32)
            p = jnp.exp(sc - jnp.max(sc, axis=-1, keepdims=True))
            p = p / jnp.sum(p, axis=-1, keepdims=True)
            heads.append(jnp.dot(p.astype(BF16), vh, preferred_element_type=F32))
        o_ref[pl.ds(r0, tl), :] = jnp.concatenate(heads, axis=1)

    @pl.when(i == n_steps - 1)
    def _project_output():
        out_ref[...] = x_ref[...] + _dot(o_ref[...], wo_ref[...])


def _cross_attn_cached(x, cache_k, cache_v, layer, w, row0, n_rows, seq_len):
    d = x.shape[1]
    ns = 4
    rows = pl.BlockSpec((n_rows, d), lambda i: (row0 // n_rows, 0), pipeline_mode=pl.Buffered(1))
    hbm = pl.BlockSpec(memory_space=pl.ANY)
    buf = pltpu.VMEM((2, ns, X_HEADS, N_MEM, X_HEAD_DIM), F32)
    return pl.pallas_call(
        functools.partial(_attn_cached_kernel, layer, ns, seq_len),
        grid=(n_rows // (ns * seq_len),),
        in_specs=[rows, _const_spec((1, d)), _const_spec((d, d)), _const_spec((d, d)), hbm, hbm],
        out_specs=pl.BlockSpec((n_rows, d), lambda i: (row0 // n_rows, 0)),
        out_shape=jax.ShapeDtypeStruct(x.shape, F32),
        scratch_shapes=[pltpu.VMEM((n_rows, d), F32), pltpu.VMEM((n_rows, d), F32), buf, buf,
                        pltpu.SemaphoreType.DMA((2,)), pltpu.SemaphoreType.DMA((2,))],
        input_output_aliases={0: 0},
        compiler_params=_params("arbitrary"),
        name="xattn_cached",
    )(x, w["g_xattn"], w["w_xq"], w["w_xo"], cache_k, cache_v)


def _topk_rows(s, k):
    n = s.shape[0]
    iota = lax.broadcasted_iota(I32, s.shape, 0).astype(F32)
    vals, idxs = [], []
    for _ in range(k):
        m = jnp.max(s, axis=0, keepdims=True)
        am = jnp.min(jnp.where(s == m, iota, float(n)), axis=0, keepdims=True)
        vals.append(m)
        idxs.append(am)
        s = jnp.where(iota == am, -jnp.inf, s)
    return jnp.concatenate(vals, axis=0), jnp.concatenate(idxs, axis=0)


def _route_kernel(x_ref, g_ref, wpq_ref, keys_ref, h_ref, i1_ref, i2_ref, gate_ref):
    h = _rms(x_ref[...], g_ref[...]).astype(BF16)
    h_ref[...] = h
    q = jnp.dot(h, wpq_ref[...], preferred_element_type=F32)
    tm = q.shape[0]
    k = PEER_TOPK
    n_cand = -(-len(_STAIR) // 8) * 8
    pad = n_cand - len(_STAIR)
    sels, gates = [], []
    for hd in range(PEER_HEADS):
        half = []
        for p in range(2):
            c0 = (hd * 2 + p) * PEER_HALF
            st = lax.dot_general(keys_ref[hd * 2 + p], q[:, c0:c0 + PEER_HALF].astype(BF16), _NT,
                                 preferred_element_type=F32)
            half.append(_topk_rows(st, k))
        (s1, i1), (s2, i2) = half
        cand, cidx = [], []
        for a in range(k):
            nb = sum(1 for (aa, _) in _STAIR if aa == a)
            cand.append(s1[a:a + 1, :] + s2[:nb, :])
            cidx.append(i1[a:a + 1, :] * float(N_KEYS) + i2[:nb, :])
        cand = jnp.concatenate(cand + [jnp.full((pad, tm), -jnp.inf, F32)], axis=0)
        cidx = jnp.concatenate(cidx + [jnp.full((pad, tm), -1.0, F32)], axis=0)
        iota = lax.broadcasted_iota(I32, cand.shape, 0).astype(F32)
        top, sel = [], []
        for _ in range(k):
            m = jnp.max(cand, axis=0, keepdims=True)
            am = jnp.min(jnp.where(cand == m, iota, float(n_cand)), axis=0, keepdims=True)
            hit = iota == am
            top.append(m)
            sel.append(jnp.max(jnp.where(hit, cidx, -1.0), axis=0, keepdims=True))
            cand = jnp.where(hit, -jnp.inf, cand)
        top = jnp.concatenate(top, axis=0)
        e = jnp.exp(top - top[0:1, :])
        gates.append(e / jnp.sum(e, axis=0, keepdims=True))
        sels.append(jnp.concatenate(sel, axis=0))
    sel = jnp.concatenate(sels, axis=0).T.astype(I32)
    i1_ref[...] = sel >> KEY_BITS
    i2_ref[...] = sel & (N_KEYS - 1)
    gate_ref[...] = jnp.concatenate(gates, axis=0).T


def _peer_route(x, w):
    t, d = x.shape
    tm = TOKEN_TILE
    nq = PEER_HEADS * 2 * PEER_HALF
    row = pl.BlockSpec((tm, d), lambda i: (i, 0))
    sel = pl.BlockSpec((tm, N_SEL), lambda i: (i, 0))
    return pl.pallas_call(
        _route_kernel,
        grid=(t // tm,),
        in_specs=[row, _const_spec((1, d)), _const_spec((d, nq)),
                  _const_spec((PEER_HEADS * 2, N_KEYS, PEER_HALF))],
        out_specs=[row, sel, sel, sel],
        out_shape=[jax.ShapeDtypeStruct((t, d), BF16), jax.ShapeDtypeStruct((t, N_SEL), I32),
                   jax.ShapeDtypeStruct((t, N_SEL), I32), jax.ShapeDtypeStruct((t, N_SEL), F32)],
        compiler_params=_params("arbitrary"),
        name="peer_route",
    )(x, w["g_ffn"], w["w_pq"], w["sub_keys"])


def _expert_kernel(split_tile, n_blocks, x_ref, h_ref, i1_ref, i2_ref, gate_ref, ua_ref, ub_ref, va_ref,
                   vb_ref, gf_ref, *refs):
    q_ref, acc_ref, a0_ref, a1_ref = refs[-4:]
    i = pl.program_id(0)
    e = pl.program_id(1)
    tm = x_ref.shape[0]
    th = ua_ref.shape[0]
    rows_per_step = th // N_KEYS
    a_refs = (a0_ref, a1_ref)

    def step(score_ref, apply_ref):
        kt = MXU_DEPTH
        if apply_ref is not None:
            r0 = (e - 1) * rows_per_step
            words = [q_ref[pl.ds(r0 + r, tm, stride=HALF_KEYS), :] for r in range(rows_per_step)]
            total = None
            for half, v_ref in enumerate((va_ref, vb_ref)):
                for k0 in range(0, th, kt):
                    wds = words[k0 // N_KEYS:(k0 + kt) // N_KEYS]
                    bits = [wd & jnp.uint32(0xFFFF0000) if half == 0 else wd << 16 for wd in wds]
                    qx = jnp.concatenate([lax.bitcast_convert_type(b, F32) for b in bits], axis=1)
                    cols = slice(half * th + k0, half * th + k0 + kt)
                    pk = _half_gated_gelu(qx, apply_ref[:, cols]).astype(BF16)
                    part = jnp.dot(pk, v_ref[k0:k0 + kt, :], preferred_element_type=F32)
                    total = part if total is None else total + part
            acc_ref[...] += total
        if score_ref is not None:
            h = h_ref[...]
            score_ref[:, :th] = lax.dot_general(h, ua_ref[...], _NT, preferred_element_type=F32)
            score_ref[:, th:] = lax.dot_general(h, ub_ref[...], _NT, preferred_element_type=F32)

    @pl.when(e == 0)
    def _build_q():
        acc_ref[...] = jnp.zeros_like(acc_ref)
        sub = lax.broadcasted_iota(I32, (N_KEYS, N_SEL), 0)

        def body(t, carry):
            i1 = i1_ref[pl.ds(t, 1), :]
            i2 = i2_ref[pl.ds(t, 1), :]
            g = 0.5 * gate_ref[pl.ds(t, 1), :]
            one_r = jnp.where(sub == i1, 1.0, 0.0).astype(BF16)
            g_c = jnp.where(sub == i2, g, 0.0).astype(BF16)
            qt = lax.dot_general(one_r, g_c, _NT, preferred_element_type=F32)
            hi = lax.bitcast_convert_type(qt[:HALF_KEYS].astype(BF16).astype(F32), U32)
            lo = lax.bitcast_convert_type(qt[HALF_KEYS:].astype(BF16).astype(F32), U32)
            q_ref[pl.ds(pl.multiple_of(t * HALF_KEYS, HALF_KEYS), HALF_KEYS), :] = hi | (lo >> 16)
            return carry

        lax.fori_loop(0, tm, body, 0, unroll=Q_BUILD_UNROLL)
        step(a_refs[0], None)

    def middle(parity):
        step(a_refs[parity], a_refs[1 - parity])

    @pl.when((e > 0) & (e < n_blocks) & (e % 2 == 0))
    def _even():
        middle(0)

    @pl.when((e > 0) & (e < n_blocks) & (e % 2 == 1))
    def _odd():
        middle(1)

    @pl.when(e == n_blocks)
    def _finish():
        step(None, a_refs[(n_blocks - 1) % 2])
        y = x_ref[...] + acc_ref[...]
        if split_tile is None:
            refs[0][...] = y
        else:
            y = _rms(y, gf_ref[...])
            first_ref, second_ref = refs[:2]

            @pl.when(i < split_tile)
            def _():
                first_ref[...] = y

            @pl.when(i >= split_tile)
            def _():
                second_ref[...] = y


def _peer_experts(x, h, i1, i2, gate, w, g_final, split_row=None):
    t, d = x.shape
    tm, th = EXPERT_TOKEN_TILE, EXPERT_TILE // 2
    n_steps = N_EXPERTS // EXPERT_TILE
    if split_row is None:
        split_tile = None
        out_specs = pl.BlockSpec((tm, d), lambda i, e: (i, 0))
        out_shape = jax.ShapeDtypeStruct((t, d), F32)
        aliases = {0: 0}
    else:
        split_tile = split_row // tm
        out_specs = [pl.BlockSpec((tm, d), lambda i, e: (jnp.minimum(i, split_tile - 1), 0)),
                     pl.BlockSpec((tm, d), lambda i, e: (jnp.maximum(i - split_tile, 0), 0))]
        out_shape = [jax.ShapeDtypeStruct((split_row, d), F32),
                     jax.ShapeDtypeStruct((t - split_row, d), F32)]
        aliases = {}
    row = pl.BlockSpec((tm, d), lambda i, e: (i, 0), pipeline_mode=pl.Buffered(1))
    sel = pl.BlockSpec((tm, N_SEL), lambda i, e: (i, 0))
    u_blk = lambda e: jnp.minimum(e, n_steps - 1)
    v_blk = lambda e: jnp.maximum(e - 1, 0)
    u_lo = pl.BlockSpec((th, d), lambda i, e: (u_blk(e), 0))
    u_hi = pl.BlockSpec((th, d), lambda i, e: (u_blk(e) + n_steps, 0))
    v_lo = pl.BlockSpec((th, d), lambda i, e: (v_blk(e), 0))
    v_hi = pl.BlockSpec((th, d), lambda i, e: (v_blk(e) + n_steps, 0))
    return pl.pallas_call(
        functools.partial(_expert_kernel, split_tile, n_steps),
        grid=(t // tm, n_steps + 1),
        in_specs=[row, row, sel, sel, sel, u_lo, u_hi, v_lo, v_hi, _const_spec((1, d))],
        out_specs=out_specs,
        out_shape=out_shape,
        scratch_shapes=[pltpu.VMEM((tm * HALF_KEYS, N_KEYS), U32), pltpu.VMEM((tm, d), F32),
                        pltpu.VMEM((tm, EXPERT_TILE), F32), pltpu.VMEM((tm, EXPERT_TILE), F32)],
        input_output_aliases=aliases,
        compiler_params=_params("arbitrary", "arbitrary"),
        name="peer_experts",
    )(x, h, i1, i2, gate, w["peer_u"], w["peer_u"], w["peer_v"], w["peer_v"], g_final)


def _spatial_gate_operands(sg_w, sg_b, c, tm):
    mask = jnp.tril(jnp.ones((c, c), dtype=bool))
    wc = jnp.where(mask, sg_w[:, :c, :c], 0)
    eye = jnp.eye(tm // c, dtype=wc.dtype)
    sgm = jnp.einsum("ab,gts->gatbs", eye, wc).reshape(SG_GROUPS, tm, tm).astype(BF16)
    bias = jnp.repeat(sg_b[:, :c].T, SG_GROUP_DIM, axis=1)
    return sgm, jnp.tile(bias, (tm // c, 1))


def kernel(x_prompt, x_sample, state_conv, cache_mem_k, cache_mem_v, mem_prompt, g_mix, w_in, conv_w, w_a_out, sg_ln_g, sg_ln_b, sg_w, sg_b, w_b_out, w_mix_o, g_xattn, g_mem, w_xq, w_xk, w_xv, w_xo, g_ffn, w_pq, sub_keys, peer_u, peer_v, g_final):
    depth = g_mix.shape[0]
    bp, lp, d = x_prompt.shape
    bs, ls, _ = x_sample.shape
    tp, ts = bp * lp, bs * ls
    x = jnp.zeros((tp + ts, d), F32)
    mem = mem_prompt.reshape(bp * N_MEM, d)
    gf = g_final.reshape(1, d)
    conv_p, mk_p, mv_p, conv_s, sgv_s = [], [], [], [], []
    for l in range(depth):
        sgm_p, sgb_p = _spatial_gate_operands(sg_w[l], sg_b[l], min(CHUNK, lp), TOKEN_TILE)
        sgm_s, sgb_s = _spatial_gate_operands(sg_w[l], sg_b[l], min(CHUNK, ls), TOKEN_TILE)
        w = dict(
            g_mix=g_mix[l].reshape(1, d), w_in=w_in[l].astype(BF16), conv_w=conv_w[l],
            w_a_out=w_a_out[l].astype(BF16), sg_ln_g=sg_ln_g[l].reshape(1, d),
            sg_ln_b=sg_ln_b[l].reshape(1, d), sgm_p=sgm_p, sgb_p=sgb_p, sgm_s=sgm_s, sgb_s=sgb_s,
            w_b_out=w_b_out[l].astype(BF16), w_mix_o=w_mix_o[l].astype(BF16),
            g_xattn=g_xattn[l].reshape(1, d), w_xq=w_xq[l].astype(BF16), w_xo=w_xo[l].astype(BF16),
            g_ffn=g_ffn[l].reshape(1, d), w_pq=w_pq[l].astype(BF16),
            sub_keys=sub_keys[l].reshape(PEER_HEADS * 2, N_KEYS, PEER_HALF).astype(BF16),
            peer_u=peer_u[l].astype(BF16), peer_v=peer_v[l].astype(BF16))

        mk, mv = _mem_kv(mem, g_mem[l].reshape(1, d), w_xk[l].astype(BF16), w_xv[l].astype(BF16))
        mk_p.append(mk.reshape(bp, N_MEM, X_HEADS, X_HEAD_DIM))
        mv_p.append(mv.reshape(bp, N_MEM, X_HEADS, X_HEAD_DIM))

        src_p, src_s = (x_prompt.reshape(tp, d), x_sample.reshape(ts, d)) if l == 0 else (None, None)
        x, nb_p, _ = _mixer(x, w, False, 0, tp, lp, x_src=src_p)
        x, nb_s, vn_s = _mixer(x, w, True, tp, ts, ls, state_conv[l], x_src=src_s)
        conv_p.append(nb_p)
        conv_s.append(nb_s)
        sgv_s.append(vn_s.reshape(bs, ls, d))
        x = _cross_attn(x, mk.reshape(bp, N_MEM, d), mv.reshape(bp, N_MEM, d), w, 0, tp, lp)
        x = _cross_attn_cached(x, cache_mem_k, cache_mem_v, l, w, tp, ts, ls)
        x = _peer_experts(x, *_peer_route(x, w), w, gf, tp if l == depth - 1 else None)

    y_prompt, y_sample = x
    return (y_prompt.reshape(bp, lp, d), y_sample.reshape(bs, ls, d), jnp.stack(conv_p), jnp.stack(mk_p),
            jnp.stack(mv_p), jnp.stack(conv_s), jnp.stack(sgv_s))
```
